```python
import jax, jax.numpy as jnp
from jax import lax
import numpy as np

D_MODEL = 2048
BATCH = 4
SEQ = 8192
DEPTH = 4

D_MIX = D_MODEL
NORM_EPS = 1e-6
NEG_INF = -1e30
SGU_WIDTH = D_MIX // 4
SGU_GROUPS = 4
SGU_GROUP_DIM = SGU_WIDTH // SGU_GROUPS
SGU_CHUNK = 128
ATT_HEADS = 4
ATT_HEAD_DIM = 128
ATT_WIDTH = ATT_HEADS * ATT_HEAD_DIM
ROPE_DIM = ATT_HEAD_DIM // 4
ROPE_THETA = 500000.0
MOBA_BLOCK = 256
MOBA_TOPK = 3
MOBA_Q_BLOCK = 64
SSM_WIDTH = D_MIX - SGU_WIDTH - ATT_WIDTH
SSM_HEAD_DIM = 64
SSM_HEADS = SSM_WIDTH // SSM_HEAD_DIM
SSM_GROUPS = 2
SSM_STATE = 128
SSM_CONV = 4
SSM_CHUNK = 128
SSM_CONV_DIM = SSM_WIDTH + 2 * SSM_GROUPS * SSM_STATE
IN_SECTIONS = (SGU_WIDTH, SGU_WIDTH, ATT_WIDTH, ATT_WIDTH, ATT_WIDTH, SSM_WIDTH, SSM_CONV_DIM, SSM_HEADS)
D_IN_PROJ = sum(IN_SECTIONS)
MOE_GROUPS = 4
MOE_EXPERTS_PER_GROUP = 8
MOE_EXPERTS = MOE_GROUPS * MOE_EXPERTS_PER_GROUP
MOE_TOPK = 2
MOE_FF = D_MODEL // 4
MOE_BLOCK = 128

kernel_name = "hybrid_sgu_moba_ssd_hmoe_trunk"


def rms_norm(x, g):
    xf = x.astype(jnp.float32)
    y = xf * lax.rsqrt(jnp.mean(xf * xf, axis=-1, keepdims=True) + NORM_EPS)
    return (y * g.astype(jnp.float32)).astype(x.dtype)


def rotary_tables(seq):
    pos = jnp.arange(seq, dtype=jnp.float32)
    inv_freq = ROPE_THETA ** (-jnp.arange(0, ROPE_DIM, 2, dtype=jnp.float32) / ROPE_DIM)
    ang = pos[:, None] * inv_freq[None, :]
    return jnp.cos(ang), jnp.sin(ang)


def apply_partial_rotary(t, cos, sin):
    half = ROPE_DIM // 2
    cos = cos.astype(t.dtype)
    sin = sin.astype(t.dtype)
    t1 = t[..., :half]
    t2 = t[..., half:ROPE_DIM]
    return jnp.concatenate([t1 * cos - t2 * sin, t2 * cos + t1 * sin, t[..., ROPE_DIM:]], axis=-1)


def chunked_spatial_gating(u, v, ln_g, ln_b, w_s, b_s):
    bsz, seq, width = u.shape
    u = jax.nn.gelu(u)
    vf = jax.nn.gelu(v).astype(jnp.float32)
    mu = jnp.mean(vf, axis=-1, keepdims=True)
    var = jnp.mean(jnp.square(vf - mu), axis=-1, keepdims=True)
    v = ((vf - mu) * lax.rsqrt(var + NORM_EPS) * ln_g + ln_b).astype(u.dtype)
    nc = seq // SGU_CHUNK
    v = v.reshape(bsz, nc, SGU_CHUNK, SGU_GROUPS, SGU_GROUP_DIM)
    w = w_s * jnp.tril(jnp.ones((SGU_CHUNK, SGU_CHUNK), w_s.dtype))
    s = jnp.einsum("gts,bnsgc->bntgc", w, v) + b_s.T[None, None, :, :, None]
    return u * s.reshape(bsz, seq, width)


def moba_attention(q, k, v):
    bsz, nh, seq, hd = q.shape
    nb = -(-seq // MOBA_BLOCK)
    pad = nb * MOBA_BLOCK - seq
    kp = jnp.pad(k, ((0, 0), (0, 0), (0, pad), (0, 0)))
    vp = jnp.pad(v, ((0, 0), (0, 0), (0, pad), (0, 0)))
    kb = kp.reshape(bsz, nh, nb, MOBA_BLOCK, hd)
    vb = vp.reshape(bsz, nh, nb, MOBA_BLOCK, hd)
    k_mean = jnp.mean(kb.astype(jnp.float32), axis=3)
    topk = min(MOBA_TOPK, nb)
    n_sel = topk * MOBA_BLOCK
    n_steps = seq // MOBA_Q_BLOCK
    q_steps = q.reshape(bsz, nh, n_steps, MOBA_Q_BLOCK, hd).transpose(2, 0, 1, 3, 4)
    b_idx = jnp.arange(bsz)[:, None, None, None]
    h_idx = jnp.arange(nh)[None, :, None, None]
    blk_ids = jnp.arange(nb)
    scale = hd ** -0.5

    def step(args):
        i, qc = args
        q0 = i * MOBA_Q_BLOCK
        own = q0 // MOBA_BLOCK
        q_pos = q0 + jnp.arange(MOBA_Q_BLOCK)
        gate = jnp.einsum("bhqd,bhnd->bhqn", qc.astype(jnp.float32), k_mean)
        gate = jnp.where(blk_ids < own, gate, NEG_INF)
        _, sel = lax.top_k(gate, topk)
        sel_ok = sel < own
        k_sel = kb[b_idx, h_idx, sel]
        v_sel = vb[b_idx, h_idx, sel]
        s_sel = jnp.einsum("bhqd,bhqjsd->bhqjs", qc, k_sel).astype(jnp.float32) * scale
        s_sel = jnp.where(sel_ok[..., None], s_sel, NEG_INF).reshape(bsz, nh, MOBA_Q_BLOCK, n_sel)
        k_own = lax.dynamic_slice_in_dim(kp, own * MOBA_BLOCK, MOBA_BLOCK, axis=2)
        v_own = lax.dynamic_slice_in_dim(vp, own * MOBA_BLOCK, MOBA_BLOCK, axis=2)
        s_own = jnp.einsum("bhqd,bhsd->bhqs", qc, k_own).astype(jnp.float32) * scale
        k_pos = own * MOBA_BLOCK + jnp.arange(MOBA_BLOCK)
        s_own = jnp.where(k_pos[None, :] <= q_pos[:, None], s_own, NEG_INF)
        p = jax.nn.softmax(jnp.concatenate([s_sel, s_own], axis=-1), axis=-1)
        p_sel = p[..., :n_sel].reshape(bsz, nh, MOBA_Q_BLOCK, topk, MOBA_BLOCK).astype(v.dtype)
        p_own = p[..., n_sel:].astype(v.dtype)
        return (jnp.einsum("bhqjs,bhqjsd->bhqd", p_sel, v_sel)
                + jnp.einsum("bhqs,bhsd->bhqd", p_own, v_own))

    out = lax.map(step, (jnp.arange(n_steps), q_steps))
    return out.transpose(1, 2, 0, 3, 4).reshape(bsz, nh, seq, hd)


def causal_depthwise_conv(x, w):
    return lax.conv_general_dilated(
        x, w[:, None, :].astype(x.dtype), window_strides=(1,), padding=[(w.shape[0] - 1, 0)],
        dimension_numbers=("NWC", "WIO", "NWC"), feature_group_count=x.shape[-1])


def ssd_chunked_scan(x, dt, a, bmat, cmat):
    bsz, seq, nh, hp = x.shape
    ng, ns = bmat.shape[2], bmat.shape[3]
    hg = nh // ng
    L = SSM_CHUNK
    nc = seq // L

    def chunks(t):
        return jnp.moveaxis(t.reshape((bsz, nc, L) + t.shape[2:]), 1, 0)

    xc = chunks(x.reshape(bsz, seq, ng, hg, hp))
    dtc = chunks(dt.reshape(bsz, seq, ng, hg))
    bc = chunks(bmat)
    cc = chunks(cmat)
    a_g = a.reshape(ng, hg)
    causal = jnp.tril(jnp.ones((L, L), bool))[None, :, :, None, None]

    def step(state, inp):
        xk, dtk, bk, ck = inp
        acum = jnp.cumsum(dtk * a_g, axis=1)
        decay = jnp.exp(jnp.where(causal, acum[:, :, None] - acum[:, None, :], -jnp.inf))
        cb = jnp.einsum("blgn,bsgn->blsg", ck, bk)
        y = jnp.einsum("blsgh,bsghp->blghp", cb[..., None] * decay * dtk[:, None], xk)
        y = y + jnp.einsum("blgn,bghpn->blghp", ck, state) * jnp.exp(acum)[..., None]
        w_end = jnp.exp(acum[:, -1:] - acum) * dtk
        state = (state * jnp.exp(acum[:, -1])[..., None, None]
                 + jnp.einsum("bsgh,bsgn,bsghp->bghpn", w_end, bk, xk))
        return state, y

    state0 = jnp.zeros((bsz, ng, hg, hp, ns), jnp.float32)
    _, y = lax.scan(step, state0, (xc, dtc, bc, cc))
    return jnp.moveaxis(y, 0, 1).reshape(bsz, seq, nh, hp)


def mamba2_group(z, xbc, dt_raw, conv_w, conv_b, dt_bias, a_log, d_skip, ssm_norm):
    bsz, seq, _ = z.shape
    xbc = jax.nn.silu(causal_depthwise_conv(xbc, conv_w) + conv_b)
    gn = SSM_GROUPS * SSM_STATE
    xs = xbc[..., :SSM_WIDTH].astype(jnp.float32).reshape(bsz, seq, SSM_HEADS, SSM_HEAD_DIM)
    bm = xbc[..., SSM_WIDTH:SSM_WIDTH + gn].astype(jnp.float32).reshape(bsz, seq, SSM_GROUPS, SSM_STATE)
    cm = xbc[..., SSM_WIDTH + gn:].astype(jnp.float32).reshape(bsz, seq, SSM_GROUPS, SSM_STATE)
    dt = jax.nn.softplus(dt_raw.astype(jnp.float32) + dt_bias.astype(jnp.float32))
    a = -jnp.exp(a_log.astype(jnp.float32))
    y = ssd_chunked_scan(xs, dt, a, bm, cm)
    y = y + d_skip.astype(jnp.float32)[:, None] * xs
    y = y.reshape(bsz, seq, SSM_WIDTH) * jax.nn.silu(z.astype(jnp.float32))
    yg = y.reshape(bsz, seq, SSM_GROUPS, SSM_WIDTH // SSM_GROUPS)
    yg = yg * lax.rsqrt(jnp.mean(yg * yg, axis=-1, keepdims=True) + NORM_EPS)
    return (yg.reshape(bsz, seq, SSM_WIDTH) * ssm_norm.astype(jnp.float32)).astype(z.dtype)


def hybrid_mixer(h, cos, sin, w_in, w_out, sgu_ln_g, sgu_ln_b, sgu_w, sgu_b,
                 conv_w, conv_b, dt_bias, a_log, d_skip, ssm_norm):
    bsz, seq, _ = h.shape
    proj = h @ w_in
    splits = [int(s) for s in np.cumsum(IN_SECTIONS)[:-1]]
    u, v, q, k, va, z, xbc, dt_raw = jnp.split(proj, splits, axis=-1)
    y_a = chunked_spatial_gating(u, v, sgu_ln_g, sgu_ln_b, sgu_w, sgu_b)
    heads = lambda t: t.reshape(bsz, seq, ATT_HEADS, ATT_HEAD_DIM).transpose(0, 2, 1, 3)
    qh = apply_partial_rotary(heads(q), cos, sin)
    kh = apply_partial_rotary(heads(k), cos, sin)
    y_b = moba_attention(qh, kh, heads(va)).transpose(0, 2, 1, 3).reshape(bsz, seq, ATT_WIDTH)
    y_c = mamba2_group(z, xbc, dt_raw, conv_w, conv_b, dt_bias, a_log, d_skip, ssm_norm)
    y = jnp.concatenate([y_a, y_b.astype(h.dtype), y_c], axis=-1)
    return y @ w_out


def hierarchical_moe(h, w_coarse, b_coarse, w_fine, b_fine, w_gate, w_up, w_down):
    bsz, seq, d = h.shape
    n_tok = bsz * seq
    ht = h.reshape(n_tok, d)
    p_group = jax.nn.softmax((ht @ w_coarse + b_coarse).astype(jnp.float32), axis=-1)
    g_prob, g_idx = lax.top_k(p_group, 1)
    fine_all = jnp.einsum("td,gde->tge", ht, w_fine) + b_fine
    fine = jnp.take_along_axis(fine_all, g_idx[:, :, None], axis=1)[:, 0]
    p_exp = jax.nn.softmax(fine.astype(jnp.float32), axis=-1)
    e_prob, e_idx = lax.top_k(p_exp, MOE_TOPK)
    gate = g_prob * e_prob / jnp.sum(e_prob, axis=-1, keepdims=True)
    expert = g_idx * MOE_EXPERTS_PER_GROUP + e_idx
    n_asg = n_tok * MOE_TOPK
    flat_e = expert.reshape(n_asg).astype(jnp.int32)
    order = jnp.argsort(flat_e)
    s_e = flat_e[order]
    s_tok = (order // MOE_TOPK).astype(jnp.int32)
    s_gate = gate.reshape(n_asg)[order]
    counts = jnp.bincount(flat_e, length=MOE_EXPERTS)
    start = jnp.cumsum(counts) - counts
    pcounts = (counts + MOE_BLOCK - 1) // MOE_BLOCK * MOE_BLOCK
    pend = jnp.cumsum(pcounts)
    pstart = pend - pcounts
    dest = pstart[s_e] + jnp.arange(n_asg) - start[s_e]
    n_rows = -(-n_asg // MOE_BLOCK) * MOE_BLOCK + MOE_EXPERTS * MOE_BLOCK
    n_blocks = n_rows // MOE_BLOCK
    row_tok = jnp.full((n_rows,), n_tok, jnp.int32).at[dest].set(s_tok)
    row_gate = jnp.zeros((n_rows,), jnp.float32).at[dest].set(s_gate)
    blk_exp = jnp.minimum(jnp.searchsorted(pend, jnp.arange(n_blocks) * MOE_BLOCK, side="right"),
                          MOE_EXPERTS - 1)
    h_pad = jnp.concatenate([ht, jnp.zeros((1, d), ht.dtype)], axis=0)

    def expert_block(args):
        e, toks = args
        xb = h_pad[toks]
        hid = jax.nn.silu(xb @ w_gate[e]) * (xb @ w_up[e])
        return hid @ w_down[e]

    y_rows = lax.map(expert_block, (blk_exp, row_tok.reshape(n_blocks, MOE_BLOCK))).reshape(n_rows, d)
    out = jnp.zeros((n_tok + 1, d), h.dtype).at[row_tok].add(y_rows * row_gate[:, None].astype(h.dtype))
    return out[:n_tok].reshape(bsz, seq, d)


def setup_inputs(seed: int = 0) -> dict:
    key = jax.random.key(seed)
    ks = jax.random.split(key, 26)
    f32 = jnp.float32

    def nrm(k, shape, s):
        return jax.random.normal(k, shape, f32) * s

    L, D = DEPTH, D_MODEL
    dt0 = jnp.exp(jax.random.uniform(ks[14], (L, SSM_HEADS), f32) * (np.log(0.1) - np.log(0.001)) + np.log(0.001))
    dt0 = jnp.maximum(dt0, 1e-4)
    return {
        "x": nrm(ks[0], (BATCH, SEQ, D), 1.0),
        "c": nrm(ks[1], (BATCH, D), 1.0),
        "norm1": 1.0 + nrm(ks[2], (L, D), 0.02),
        "norm2": 1.0 + nrm(ks[3], (L, D), 0.02),
        "w_ada": nrm(ks[4], (L, D, 6 * D), 0.5 * D ** -0.5),
        "b_ada": nrm(ks[5], (L, 6 * D), 0.02),
        "w_in": nrm(ks[6], (L, D, D_IN_PROJ), D ** -0.5),
        "w_out": nrm(ks[7], (L, D_MIX, D), D_MIX ** -0.5),
        "sgu_ln_g": 1.0 + nrm(ks[8], (L, SGU_WIDTH), 0.02),
        "sgu_ln_b": nrm(ks[9], (L, SGU_WIDTH), 0.02),
        "sgu_w": nrm(ks[10], (L, SGU_GROUPS, SGU_CHUNK, SGU_CHUNK), SGU_CHUNK ** -0.5),
        "sgu_b": 1.0 + nrm(ks[11], (L, SGU_GROUPS, SGU_CHUNK), 0.02),
        "conv_w": nrm(ks[12], (L, SSM_CONV, SSM_CONV_DIM), SSM_CONV ** -0.5),
        "conv_b": nrm(ks[13], (L, SSM_CONV_DIM), 0.02),
        "dt_bias": dt0 + jnp.log(-jnp.expm1(-dt0)),
        "a_log": jnp.log(jax.random.uniform(ks[15], (L, SSM_HEADS), f32, 1.0, 16.0)),
        "d_skip": 1.0 + nrm(ks[16], (L, SSM_HEADS), 0.02),
        "ssm_norm": 1.0 + nrm(ks[17], (L, SSM_WIDTH), 0.02),
        "w_coarse": nrm(ks[18], (L, D, MOE_GROUPS), D ** -0.5),
        "b_coarse": nrm(ks[19], (L, MOE_GROUPS), 0.01),
        "w_fine": nrm(ks[20], (L, MOE_GROUPS, D, MOE_EXPERTS_PER_GROUP), D ** -0.5),
        "b_fine": nrm(ks[21], (L, MOE_GROUPS, MOE_EXPERTS_PER_GROUP), 0.01),
        "w_gate": nrm(ks[22], (L, MOE_EXPERTS, D, MOE_FF), D ** -0.5),
        "w_up": nrm(ks[23], (L, MOE_EXPERTS, D, MOE_FF), D ** -0.5),
        "w_down": nrm(ks[24], (L, MOE_EXPERTS, MOE_FF, D), MOE_FF ** -0.5),
        "final_norm": 1.0 + nrm(ks[25], (D,), 0.02),
    }


def reference(x, c, norm1, norm2, w_ada, b_ada, w_in, w_out, sgu_ln_g, sgu_ln_b, sgu_w, sgu_b,
              conv_w, conv_b, dt_bias, a_log, d_skip, ssm_norm, w_coarse, b_coarse, w_fine, b_fine,
              w_gate, w_up, w_down, final_norm):
    bsz, seq, _ = x.shape
    cos, sin = rotary_tables(seq)
    c_act = jax.nn.silu(c)
    for l in range(DEPTH):
        mod = (c_act @ w_ada[l] + b_ada[l]).reshape(bsz, 6, D_MODEL)
        sh1, sc1, g1, sh2, sc2, g2 = [mod[:, i, None, :] for i in range(6)]
        h = rms_norm(x, norm1[l]) * (1.0 + sc1) + sh1
        x = x + g1 * hybrid_mixer(h, cos, sin, w_in[l], w_out[l], sgu_ln_g[l], sgu_ln_b[l], sgu_w[l], sgu_b[l],
                                  conv_w[l], conv_b[l], dt_bias[l], a_log[l], d_skip[l], ssm_norm[l])
        h = rms_norm(x, norm2[l]) * (1.0 + sc2) + sh2
        x = x + g2 * hierarchical_moe(h, w_coarse[l], b_coarse[l], w_fine[l], b_fine[l],
                                      w_gate[l], w_up[l], w_down[l])
    return rms_norm(x, final_norm)
```

```python
import functools

import jax
import jax.numpy as jnp
from jax import lax
from jax.experimental import pallas as pl
from jax.experimental.pallas import tpu as pltpu

F32 = jnp.float32
BF16 = jnp.bfloat16

LANES = 128
NORM_EPS = 1e-6
NEG_INF = -1e30

D_MODEL = 2048
N_CHUNK = D_MODEL // LANES
SGU_WIDTH = 512
SGU_GROUPS = 4
SGU_CHUNK = 128
ATT_HEADS = 4
ATT_HEAD_DIM = 128
ROPE_DIM = 32
ROPE_THETA = 500000.0
MOBA_BLOCK = 256
MOBA_TOPK = 3
SSM_WIDTH = 1024
SSM_HEAD_DIM = 64
SSM_HEADS = 16
SSM_GROUPS = 2
SSM_STATE = 128
SSM_CONV = 4
SSM_CHUNK = 128
SSM_CONV_DIM = 1536
MOE_GROUPS = 4
MOE_EPG = 8
MOE_EXPERTS = 32
MOE_FF = 512
MOE_BLOCK = 128

PROJ_MAIN = 5120
OFF_U, OFF_Z, OFF_V, OFF_Q, OFF_K, OFF_VA = 1536, 2048, 3072, 3584, 4096, 4608

VMEM_LIMIT = 56 * 1024 * 1024


def _params(sem, vmem=VMEM_LIMIT):
    return pltpu.CompilerParams(dimension_semantics=sem, vmem_limit_bytes=vmem)


def _silu(x):
    return x * jax.nn.sigmoid(x)


def _ada_kernel(c_ref, w_ref, b_ref, o_ref):
    ca = _silu(c_ref[...]).astype(BF16)
    o_ref[0] = jnp.dot(ca, w_ref[0].astype(BF16), preferred_element_type=F32) + b_ref[0]


def _ada_mod(c_pad, w_ada, b_ada):
    depth, d, n = w_ada.shape
    tn = 1024
    return pl.pallas_call(
        _ada_kernel,
        out_shape=jax.ShapeDtypeStruct((depth, c_pad.shape[0], n), F32),
        grid=(depth, n // tn),
        in_specs=[
            pl.BlockSpec(c_pad.shape, lambda l, j: (0, 0)),
            pl.BlockSpec((1, d, tn), lambda l, j: (l, 0, j)),
            pl.BlockSpec((1, 1, tn), lambda l, j: (l, 0, j)),
        ],
        out_specs=pl.BlockSpec((1, c_pad.shape[0], tn), lambda l, j: (l, 0, j)),
        compiler_params=_params(("parallel", "parallel")),
        name="ada_mod",
    )(c_pad, w_ada, b_ada.reshape(depth, 1, n))


def _modulated_norm(x, g, sc, sh):
    ms = jnp.mean(x * x, axis=-1, keepdims=True)
    return (x * lax.rsqrt(ms + NORM_EPS)) * g * (1.0 + sc) + sh


def _inproj_kernel(x_ref, g_ref, sc_ref, sh_ref, w_ref, wdt_ref, o_ref, dt_ref, h_ref):
    @pl.when(pl.program_id(2) == 0)
    def _():
        h = _modulated_norm(x_ref[0], g_ref[...], sc_ref[0], sh_ref[0]).astype(BF16)
        h_ref[...] = h
        dt_ref[0] = jnp.dot(h, wdt_ref[...], preferred_element_type=F32)

    o_ref[0] = jnp.dot(h_ref[...], w_ref[...], preferred_element_type=F32)


def _in_proj(x, g, sc, sh, w, wdt):
    bsz, seq, d = x.shape
    n = w.shape[1]
    tm, tn = 512, 1024
    return pl.pallas_call(
        _inproj_kernel,
        out_shape=(jax.ShapeDtypeStruct((bsz, seq, n), F32),
                   jax.ShapeDtypeStruct((bsz, seq, LANES), F32)),
        grid=(bsz, seq // tm, n // tn),
        in_specs=[
            pl.BlockSpec((1, tm, d), lambda b, i, j: (b, i, 0)),
            pl.BlockSpec((1, d), lambda b, i, j: (0, 0)),
            pl.BlockSpec((1, 1, d), lambda b, i, j: (b, 0, 0)),
            pl.BlockSpec((1, 1, d), lambda b, i, j: (b, 0, 0)),
            pl.BlockSpec((d, tn), lambda b, i, j: (0, j)),
            pl.BlockSpec((d, LANES), lambda b, i, j: (0, 0)),
        ],
        out_specs=(pl.BlockSpec((1, tm, tn), lambda b, i, j: (b, i, j)),
                   pl.BlockSpec((1, tm, LANES), lambda b, i, j: (b, i, 0))),
        scratch_shapes=[pltpu.VMEM((tm, d), BF16)],
        compiler_params=_params(("parallel", "parallel", "arbitrary")),
        name="in_proj",
    )(x, g, sc, sh, w, wdt)


def _sgu_kernel(u_ref, v_ref, g_ref, b_ref, w_ref, bs_ref, o_ref, *, n_chunks):
    row = lax.broadcasted_iota(jnp.int32, (SGU_CHUNK, SGU_CHUNK), 0)
    col = lax.broadcasted_iota(jnp.int32, (SGU_CHUNK, SGU_CHUNK), 1)
    w_causal = [jnp.where(row >= col, w_ref[g], 0.0).astype(BF16) for g in range(SGU_GROUPS)]
    for ci in range(n_chunks):
        rows = slice(ci * SGU_CHUNK, (ci + 1) * SGU_CHUNK)
        u = jax.nn.gelu(u_ref[0, rows, :])
        vf = jax.nn.gelu(v_ref[0, rows, :])
        mu = jnp.mean(vf, axis=-1, keepdims=True)
        var = jnp.mean(jnp.square(vf - mu), axis=-1, keepdims=True)
        vn = ((vf - mu) * lax.rsqrt(var + NORM_EPS) * g_ref[...] + b_ref[...]).astype(BF16)
        for g in range(SGU_GROUPS):
            cols = slice(g * LANES, (g + 1) * LANES)
            s = jnp.dot(w_causal[g], vn[:, cols], preferred_element_type=F32) + bs_ref[:, g:g + 1]
            o_ref[0, rows, cols] = (u[:, cols] * s).astype(BF16)


def _sgu(proj, ln_g, ln_b, w_s, b_s_t):
    bsz, seq, _ = proj.shape
    tc = 512
    blk = lambda off: pl.BlockSpec((1, tc, SGU_WIDTH), lambda b, i: (b, i, off // SGU_WIDTH))
    return pl.pallas_call(
        functools.partial(_sgu_kernel, n_chunks=tc // SGU_CHUNK),
        out_shape=jax.ShapeDtypeStruct((bsz, seq, SGU_WIDTH), BF16),
        grid=(bsz, seq // tc),
        in_specs=[
            blk(OFF_U), blk(OFF_V),
            pl.BlockSpec((1, SGU_WIDTH), lambda b, i: (0, 0)),
            pl.BlockSpec((1, SGU_WIDTH), lambda b, i: (0, 0)),
            pl.BlockSpec((SGU_GROUPS, SGU_CHUNK, SGU_CHUNK), lambda b, i: (0, 0, 0)),
            pl.BlockSpec((SGU_CHUNK, SGU_GROUPS), lambda b, i: (0, 0)),
        ],
        out_specs=pl.BlockSpec((1, tc, SGU_WIDTH), lambda b, i: (b, i, 0)),
        compiler_params=_params(("parallel", "parallel")),
        name="sgu",
    )(proj, proj, ln_g, ln_b, w_s, b_s_t)


def _rotary(t, c, s1, s2):
    half = ROPE_DIM // 2
    return t * c + pltpu.roll(t, LANES - half, 1) * s1 + pltpu.roll(t, half, 1) * s2


def _rope_kernel(q_ref, k_ref, v_ref, c_ref, s1_ref, s2_ref, qt_ref, kr_ref, vt_ref, km_ref):
    c, s1, s2 = c_ref[...], s1_ref[...], s2_ref[...]
    q = _rotary(q_ref[0], c, s1, s2)
    k = _rotary(k_ref[0], c, s1, s2)
    qt_ref[0, 0, 0] = q.T
    kr_ref[0, 0, 0] = k.astype(BF16)
    vt_ref[0, 0, 0] = v_ref[0].T.astype(BF16)
    km_ref[0, 0, 0] = jnp.mean(k, axis=0, keepdims=True)


def _rope_prep(proj, tab_c, tab_s1, tab_s2):
    bsz, seq, _ = proj.shape
    nblk = seq // MOBA_BLOCK
    hd = ATT_HEAD_DIM
    sec = lambda off: pl.BlockSpec((1, MOBA_BLOCK, hd), lambda b, h, i: (b, i, off // hd + h))
    tab = pl.BlockSpec((MOBA_BLOCK, hd), lambda b, h, i: (i, 0))
    return pl.pallas_call(
        _rope_kernel,
        out_shape=(
            jax.ShapeDtypeStruct((bsz, ATT_HEADS, nblk, hd, MOBA_BLOCK), F32),
            jax.ShapeDtypeStruct((bsz, ATT_HEADS, nblk, MOBA_BLOCK, hd), BF16),
            jax.ShapeDtypeStruct((bsz, ATT_HEADS, nblk, hd, MOBA_BLOCK), BF16),
            jax.ShapeDtypeStruct((bsz, ATT_HEADS, nblk, 1, hd), F32),
        ),
        grid=(bsz, ATT_HEADS, nblk),
        in_specs=[sec(OFF_Q), sec(OFF_K), sec(OFF_VA), tab, tab, tab],
        out_specs=(
            pl.BlockSpec((1, 1, 1, hd, MOBA_BLOCK), lambda b, h, i: (b, h, i, 0, 0)),
            pl.BlockSpec((1, 1, 1, MOBA_BLOCK, hd), lambda b, h, i: (b, h, i, 0, 0)),
            pl.BlockSpec((1, 1, 1, hd, MOBA_BLOCK), lambda b, h, i: (b, h, i, 0, 0)),
            pl.BlockSpec((1, 1, 1, 1, hd), lambda b, h, i: (b, h, i, 0, 0)),
        ),
        compiler_params=_params(("parallel", "parallel", "parallel")),
        name="rope_prep",
    )(proj, proj, proj, tab_c, tab_s1, tab_s2)


def _moba_kernel(qt_ref, kr_ref, vt_ref, km_ref, o_ref, sel_ref, *, nblk):
    i = pl.program_id(2)
    tq = MOBA_BLOCK
    scale = ATT_HEAD_DIM ** -0.5
    qt = qt_ref[0, 0, 0]
    qtb = qt.astype(BF16)

    gate = jnp.dot(km_ref[0, 0], qt, preferred_element_type=F32,
                   precision=lax.Precision.HIGHEST)
    blk = lax.broadcasted_iota(jnp.int32, (nblk, tq), 0)
    valid = blk < i
    gate = jnp.where(valid, gate, NEG_INF)
    cnt = jnp.zeros((nblk, tq), F32)
    for jp in range(nblk):
        gj = gate[jp:jp + 1, :]
        beats = (gj > gate) | ((gj == gate) & (blk > jp))
        cnt = cnt + jnp.where(beats, 1.0, 0.0)
    sel_ref[...] = jnp.where(valid & (cnt < MOBA_TOPK), 1.0, 0.0)

    krow = lax.broadcasted_iota(jnp.int32, (tq, tq), 0)
    qcol = lax.broadcasted_iota(jnp.int32, (tq, tq), 1)
    s = jnp.dot(kr_ref[0, 0, i], qtb, preferred_element_type=F32) * scale
    s = jnp.where(krow <= qcol, s, NEG_INF)
    m0 = jnp.max(s, axis=0, keepdims=True)
    p = jnp.exp(s - m0)
    l0 = jnp.sum(p, axis=0, keepdims=True)
    acc0 = jnp.dot(vt_ref[0, 0, i], p.astype(BF16), preferred_element_type=F32)

    def body(j, carry):
        m, l, acc = carry
        s = jnp.dot(kr_ref[0, 0, j], qtb, preferred_element_type=F32) * scale
        s = jnp.where(sel_ref[pl.ds(j, 1), :] > 0.5, s, NEG_INF)
        m_new = jnp.maximum(m, jnp.max(s, axis=0, keepdims=True))
        alpha = jnp.exp(m - m_new)
        p = jnp.exp(s - m_new)
        l = alpha * l + jnp.sum(p, axis=0, keepdims=True)
        acc = alpha * acc + jnp.dot(vt_ref[0, 0, j], p.astype(BF16), preferred_element_type=F32)
        return m_new, l, acc

    _, l, acc = lax.fori_loop(0, i, body, (m0, l0, acc0))
    o_ref[0] = (acc / l).T.astype(BF16)


def _moba(qt, kr, vt, km):
    bsz, nh, nblk, hd, tq = qt.shape
    seq = nblk * tq
    full = lambda shape: pl.BlockSpec((1, 1) + shape, lambda b, h, i: (b, h) + (0,) * len(shape))
    return pl.pallas_call(
        functools.partial(_moba_kernel, nblk=nblk),
        out_shape=jax.ShapeDtypeStruct((bsz, seq, nh * hd), BF16),
        grid=(bsz, nh, nblk),
        in_specs=[
            pl.BlockSpec((1, 1, 1, hd, tq), lambda b, h, i: (b, h, i, 0, 0)),
            full((nblk, tq, hd)), full((nblk, hd, tq)), full((nblk, hd)),
        ],
        out_specs=pl.BlockSpec((1, tq, hd), lambda b, h, i: (b, i, h)),
        scratch_shapes=[pltpu.VMEM((nblk, tq), F32)],
        compiler_params=_params(("parallel", "parallel", "arbitrary")),
        name="moba",
    )(qt, kr, vt, km)


def _softplus(x):
    return jnp.maximum(x, 0.0) + jnp.log1p(jnp.exp(-jnp.abs(x)))


def _ssd_kernel(xbc_ref, z_ref, dt_ref, cw_ref, cb_ref, dtb_ref, alog_ref, dsk_ref, nw_ref,
                o_ref, prev_ref, state_ref, y_ref):
    L, P, N = SSM_CHUNK, SSM_HEAD_DIM, SSM_STATE
    hi = lax.Precision.HIGHEST

    @pl.when(pl.program_id(1) == 0)
    def _():
        prev_ref[...] = jnp.zeros_like(prev_ref)
        state_ref[...] = jnp.zeros_like(state_ref)

    cur = xbc_ref[0]
    prev = prev_ref[...]
    crow = lax.broadcasted_iota(jnp.int32, cur.shape, 0)
    acc = cur * cw_ref[SSM_CONV - 1:SSM_CONV, :] + cb_ref[...]
    for j in range(1, SSM_CONV):
        shifted = pltpu.roll(jnp.where(crow >= L - j, prev, cur), j, 0)
        acc = acc + shifted * cw_ref[SSM_CONV - 1 - j:SSM_CONV - j, :]
    prev_ref[...] = cur
    xc = _silu(acc)
    xs = xc[:, :SSM_WIDTH]
    bm = xc[:, SSM_WIDTH:SSM_WIDTH + SSM_GROUPS * N].astype(BF16)
    cm = xc[:, SSM_WIDTH + SSM_GROUPS * N:].astype(BF16)
    xs_b = xs.astype(BF16)
    xs_t = xs.T

    dt = _softplus(dt_ref[0] + dtb_ref[...])
    da = dt * (-jnp.exp(alog_ref[...]))
    row = lax.broadcasted_iota(jnp.int32, (L, L), 0)
    col = lax.broadcasted_iota(jnp.int32, (L, L), 1)
    causal = row >= col
    acum = jnp.dot(jnp.where(causal, 1.0, 0.0), da, preferred_element_type=F32, precision=hi)
    acum_t = jnp.dot(da.T, jnp.where(row <= col, 1.0, 0.0), preferred_element_type=F32, precision=hi)
    dt_t = dt.T
    last_t = acum_t[:, L - 1:L]
    w_end_t = jnp.exp(last_t - acum_t) * dt_t
    e_last_t = jnp.exp(last_t)
    e_acum = jnp.exp(acum)

    for g in range(SSM_GROUPS):
        b_g = bm[:, g * N:(g + 1) * N]
        c_g = cm[:, g * N:(g + 1) * N]
        cb = lax.dot_general(c_g, b_g, (((1,), (1,)), ((), ())), preferred_element_type=F32)
        for hh in range(SSM_HEADS // SSM_GROUPS):
            h = g * (SSM_HEADS // SSM_GROUPS) + hh
            hs = slice(h * P, (h + 1) * P)
            decay = jnp.exp(jnp.where(causal, acum[:, h:h + 1] - acum_t[h:h + 1, :], -jnp.inf))
            mat = (cb * decay * dt_t[h:h + 1, :]).astype(BF16)
            y = jnp.dot(mat, xs_b[:, hs], preferred_element_type=F32)
            st = state_ref[hs, :]
            y = y + lax.dot_general(c_g, st.astype(BF16), (((1,), (1,)), ((), ())),
                                    preferred_element_type=F32) * e_acum[:, h:h + 1]
            y_ref[:, hs] = y
            xw = (xs_t[hs, :] * w_end_t[h:h + 1, :]).astype(BF16)
            state_ref[hs, :] = st * e_last_t[h:h + 1, :] + jnp.dot(xw, b_g, preferred_element_type=F32)

    y = y_ref[...] + dsk_ref[...] * xs
    y = y * _silu(z_ref[0])
    gw = SSM_WIDTH // SSM_GROUPS
    for g in range(SSM_GROUPS):
        cols = slice(g * gw, (g + 1) * gw)
        yg = y[:, cols]
        yg = yg * lax.rsqrt(jnp.mean(yg * yg, axis=-1, keepdims=True) + NORM_EPS)
        o_ref[0, :, cols] = (yg * nw_ref[:, cols]).astype(BF16)


def _ssd(proj, dt_raw, conv_w, conv_b, dt_bias, a_log, d_skip, ssm_norm):
    bsz, seq, _ = proj.shape
    L = SSM_CHUNK
    vec = lambda n: pl.BlockSpec((1, n), lambda b, c: (0, 0))
    return pl.pallas_call(
        _ssd_kernel,
        out_shape=jax.ShapeDtypeStruct((bsz, seq, SSM_WIDTH), BF16),
        grid=(bsz, seq // L),
        in_specs=[
            pl.BlockSpec((1, L, SSM_CONV_DIM), lambda b, c: (b, c, 0)),
            pl.BlockSpec((1, L, SSM_WIDTH), lambda b, c: (b, c, OFF_Z // SSM_WIDTH)),
            pl.BlockSpec((1, L, LANES), lambda b, c: (b, c, 0)),
            pl.BlockSpec((SSM_CONV, SSM_CONV_DIM), lambda b, c: (0, 0)),
            vec(SSM_CONV_DIM), vec(LANES), vec(LANES), vec(SSM_WIDTH), vec(SSM_WIDTH),
        ],
        out_specs=pl.BlockSpec((1, L, SSM_WIDTH), lambda b, c: (b, c, 0)),
        scratch_shapes=[
            pltpu.VMEM((L, SSM_CONV_DIM), F32),
            pltpu.VMEM((SSM_WIDTH, SSM_STATE), F32),
            pltpu.VMEM((L, SSM_WIDTH), F32),
        ],
        compiler_params=_params(("parallel", "arbitrary")),
        name="ssd",
    )(proj, proj, dt_raw, conv_w, conv_b, dt_bias, a_log, d_skip, ssm_norm)


def _route(logits):
    lane = lax.broadcasted_iota(jnp.int32, logits.shape, 1).astype(F32)
    big = float(LANES)

    def first_argmax(vals, mask):
        top = jnp.max(vals, axis=-1, keepdims=True)
        idx = jnp.min(jnp.where(mask & (vals >= top), lane, big), axis=-1, keepdims=True)
        return top, idx

    def masked_softmax(mask):
        z = jnp.where(mask, logits, -jnp.inf)
        e = jnp.exp(z - jnp.max(z, axis=-1, keepdims=True))
        return e / jnp.sum(e, axis=-1, keepdims=True)

    gmask = lane < MOE_GROUPS
    g_prob, g_idx = first_argmax(jnp.where(gmask, masked_softmax(gmask), -1.0), gmask)
    lo = MOE_GROUPS + g_idx * MOE_EPG
    fmask = (lane >= lo) & (lane < lo + MOE_EPG)
    p_exp = jnp.where(fmask, masked_softmax(fmask), -1.0)
    p1, i1 = first_argmax(p_exp, fmask)
    rest = fmask & (lane != i1)
    p2, i2 = first_argmax(jnp.where(rest, p_exp, -1.0), rest)
    den = p1 + p2
    out = jnp.where(lane == 0, i1 - MOE_GROUPS, 0.0)
    out = jnp.where(lane == 1, i2 - MOE_GROUPS, out)
    out = jnp.where(lane == 2, g_prob * p1 / den, out)
    out = jnp.where(lane == 3, g_prob * p2 / den, out)
    return out


def _outproj_kernel(ya_ref, yb_ref, yc_ref, x_ref, g1_ref, wa_ref, wb_ref, wc_ref,
                    n2_ref, sc_ref, sh_ref, wr_ref, br_ref, xo_ref, hc_ref, rt_ref):
    mix = jnp.dot(ya_ref[0], wa_ref[...], preferred_element_type=F32)
    mix = mix + jnp.dot(yb_ref[0], wb_ref[...], preferred_element_type=F32)
    mix = mix + jnp.dot(yc_ref[0], wc_ref[...], preferred_element_type=F32)
    xn = x_ref[0] + g1_ref[0] * mix
    xo_ref[0] = xn
    h = _modulated_norm(xn, n2_ref[...], sc_ref[0], sh_ref[0])
    for c in range(N_CHUNK):
        hc_ref[0, c] = h[:, c * LANES:(c + 1) * LANES]
    logits = jnp.dot(h, wr_ref[...], preferred_element_type=F32,
                     precision=lax.Precision.HIGHEST) + br_ref[...]
    rt_ref[0] = _route(logits)


TM_TOK = 256


def _out_proj_route(ya, yb, yc, x, g1, wa, wb, wc, n2, sc2, sh2, w_r, b_r):
    bsz, seq, d = x.shape
    tm = TM_TOK
    nt = seq // tm
    row = lambda w: pl.BlockSpec((1, tm, w), lambda b, i: (b, i, 0))
    per_b = pl.BlockSpec((1, 1, d), lambda b, i: (b, 0, 0))
    const = lambda a: pl.BlockSpec(a.shape, lambda b, i: (0,) * a.ndim)
    return pl.pallas_call(
        _outproj_kernel,
        out_shape=(jax.ShapeDtypeStruct((bsz, seq, d), F32),
                   jax.ShapeDtypeStruct((bsz * nt, N_CHUNK, tm, LANES), F32),
                   jax.ShapeDtypeStruct((bsz, seq, LANES), F32)),
        grid=(bsz, nt),
        in_specs=[row(ya.shape[-1]), row(yb.shape[-1]), row(yc.shape[-1]), row(d), per_b,
                  const(wa), const(wb), const(wc), const(n2), per_b, per_b, const(w_r), const(b_r)],
        out_specs=(row(d),
                   pl.BlockSpec((1, N_CHUNK, tm, LANES), lambda b, i: (b * nt + i, 0, 0, 0)),
                   row(LANES)),
        compiler_params=_params(("parallel", "parallel")),
        name="out_proj_route",
    )(ya, yb, yc, x, g1, wa, wb, wc, n2, sc2, sh2, w_r, b_r)


BLOCK_SHIFT = MOE_BLOCK.bit_length() - 1


def _dispatch_kernel(dest_ref, h_hbm, init_hbm, xs_hbm, sem):
    del init_hbm
    i = pl.program_id(0)
    tm = TM_TOK

    def scatter(k, mi):
        d = dest_ref[(i * tm + mi) * 2 + k]
        return pltpu.make_async_copy(
            h_hbm.at[i, :, mi, :],
            xs_hbm.at[jnp.right_shift(d, BLOCK_SHIFT), :, jnp.bitwise_and(d, MOE_BLOCK - 1), :], sem)

    def issue(mi, carry):
        for k in range(2):
            scatter(k, mi).start()
        return carry

    def drain(mi, carry):
        for k in range(2):
            scatter(k, mi).wait()
        return carry

    lax.fori_loop(0, tm, issue, 0)
    lax.fori_loop(0, tm, drain, 0)


def _dispatch(dest_flat, h_chunks, n_blocks):
    nt = h_chunks.shape[0]
    init = jnp.zeros((n_blocks, N_CHUNK, MOE_BLOCK, LANES), F32)
    return pl.pallas_call(
        _dispatch_kernel,
        out_shape=jax.ShapeDtypeStruct(init.shape, F32),
        grid_spec=pltpu.PrefetchScalarGridSpec(
            num_scalar_prefetch=1,
            grid=(nt,),
            in_specs=[pl.BlockSpec(memory_space=pl.ANY), pl.BlockSpec(memory_space=pl.ANY)],
            out_specs=pl.BlockSpec(memory_space=pl.ANY),
            scratch_shapes=[pltpu.SemaphoreType.DMA],
        ),
        input_output_aliases={2: 0},
        compiler_params=_params(("arbitrary",)),
        name="moe_dispatch",
    )(dest_flat, h_chunks, init)


def _expert_kernel(be_ref, xs_ref, wg_ref, wu_ref, wd_ref, o_ref, wgb_ref, wub_ref, wdb_ref):
    i = pl.program_id(0)
    prev = be_ref[jnp.maximum(i - 1, 0)]

    @pl.when((i == 0) | (be_ref[i] != prev))
    def _():
        wgb_ref[...] = wg_ref[0].astype(BF16)
        wub_ref[...] = wu_ref[0].astype(BF16)
        wdb_ref[...] = wd_ref[0].astype(BF16)

    x = jnp.concatenate([xs_ref[0, c] for c in range(N_CHUNK)], axis=-1).astype(BF16)
    gate = jnp.dot(x, wgb_ref[...], preferred_element_type=F32)
    up = jnp.dot(x, wub_ref[...], preferred_element_type=F32)
    hid = (_silu(gate) * up).astype(BF16)
    y = jnp.dot(hid, wdb_ref[...], preferred_element_type=F32)
    for c in range(N_CHUNK):
        o_ref[0, c] = y[:, c * LANES:(c + 1) * LANES]


def _experts(blk_exp, xs, w_gate, w_up, w_down):
    n_blocks = xs.shape[0]
    d, ff = w_gate.shape[1], w_gate.shape[2]
    blk = pl.BlockSpec((1, N_CHUNK, MOE_BLOCK, LANES), lambda i, be: (i, 0, 0, 0))
    return pl.pallas_call(
        _expert_kernel,
        out_shape=jax.ShapeDtypeStruct(xs.shape, F32),
        grid_spec=pltpu.PrefetchScalarGridSpec(
            num_scalar_prefetch=1,
            grid=(n_blocks,),
            in_specs=[blk,
                      pl.BlockSpec((1, d, ff), lambda i, be: (be[i], 0, 0)),
                      pl.BlockSpec((1, d, ff), lambda i, be: (be[i], 0, 0)),
                      pl.BlockSpec((1, ff, d), lambda i, be: (be[i], 0, 0))],
            out_specs=blk,
            scratch_shapes=[pltpu.VMEM((d, ff), BF16), pltpu.VMEM((d, ff), BF16),
                            pltpu.VMEM((ff, d), BF16)],
        ),
        compiler_params=_params(("arbitrary",)),
        name="moe_experts",
    )(blk_exp, xs, w_gate, w_up, w_down)


def _combine_kernel(dest_ref, y_hbm, x_ref, g2_ref, rt_ref, o_ref, buf_ref, sem):
    i = pl.program_id(0)
    tm = TM_TOK

    def gather(k, mi):
        d = dest_ref[(i * tm + mi) * 2 + k]
        return pltpu.make_async_copy(
            y_hbm.at[jnp.right_shift(d, BLOCK_SHIFT), :, jnp.bitwise_and(d, MOE_BLOCK - 1), :],
            buf_ref.at[k, :, mi, :], sem)

    def issue(mi, carry):
        for k in range(2):
            gather(k, mi).start()
        return carry

    def drain(mi, carry):
        for k in range(2):
            gather(k, mi).wait()
        return carry

    lax.fori_loop(0, tm, issue, 0)
    lax.fori_loop(0, tm, drain, 0)
    rt = rt_ref[...]
    gate0, gate1 = rt[:, 2:3], rt[:, 3:4]
    for c in range(N_CHUNK):
        cols = slice(c * LANES, (c + 1) * LANES)
        moe = buf_ref[0, c] * gate0 + buf_ref[1, c] * gate1
        o_ref[:, cols] = x_ref[:, cols] + g2_ref[0, :, cols] * moe


def _combine(dest_flat, y_rows, x2d, g2, route2d, seq):
    n_tok, d = x2d.shape
    tm = TM_TOK
    per_seq = seq // tm
    return pl.pallas_call(
        _combine_kernel,
        out_shape=jax.ShapeDtypeStruct((n_tok, d), F32),
        grid_spec=pltpu.PrefetchScalarGridSpec(
            num_scalar_prefetch=1,
            grid=(n_tok // tm,),
            in_specs=[pl.BlockSpec(memory_space=pl.ANY),
                      pl.BlockSpec((tm, d), lambda i, dr: (i, 0)),
                      pl.BlockSpec((1, 1, d), lambda i, dr: (i // per_seq, 0, 0)),
                      pl.BlockSpec((tm, LANES), lambda i, dr: (i, 0))],
            out_specs=pl.BlockSpec((tm, d), lambda i, dr: (i, 0)),
            scratch_shapes=[pltpu.VMEM((2, N_CHUNK, tm, LANES), F32), pltpu.SemaphoreType.DMA],
        ),
        compiler_params=_params(("arbitrary",)),
        name="moe_combine",
    )(dest_flat, y_rows, x2d, g2, route2d)


def _dispatch_plan(route2d):
    n_tok = route2d.shape[0]
    n_asg = n_tok * 2
    flat_e = route2d[:, 0:2].astype(jnp.int32).reshape(n_asg)
    order = jnp.argsort(flat_e)
    s_e = flat_e[order]
    ids = jnp.arange(MOE_EXPERTS, dtype=jnp.int32)
    start = jnp.searchsorted(s_e, ids, side="left").astype(jnp.int32)
    end = jnp.searchsorted(s_e, ids, side="right").astype(jnp.int32)
    pcounts = (end - start + MOE_BLOCK - 1) // MOE_BLOCK * MOE_BLOCK
    pend = jnp.cumsum(pcounts)
    pstart = pend - pcounts
    dest_sorted = pstart[s_e] + jnp.arange(n_asg, dtype=jnp.int32) - start[s_e]
    dest = jnp.zeros((n_asg,), jnp.int32).at[order].set(dest_sorted.astype(jnp.int32))
    n_blocks = n_asg // MOE_BLOCK + MOE_EXPERTS
    blk_exp = jnp.minimum(
        jnp.searchsorted(pend, jnp.arange(n_blocks, dtype=jnp.int32) * MOE_BLOCK, side="right"),
        MOE_EXPERTS - 1).astype(jnp.int32)
    return dest, blk_exp, n_blocks


def _final_norm_kernel(x_ref, g_ref, o_ref):
    x = x_ref[...]
    ms = jnp.mean(x * x, axis=-1, keepdims=True)
    o_ref[...] = (x * lax.rsqrt(ms + NORM_EPS)) * g_ref[...]


def _final_norm(x2d, g):
    n_tok, d = x2d.shape
    tm = 512
    return pl.pallas_call(
        _final_norm_kernel,
        out_shape=jax.ShapeDtypeStruct((n_tok, d), F32),
        grid=(n_tok // tm,),
        in_specs=[pl.BlockSpec((tm, d), lambda i: (i, 0)), pl.BlockSpec((1, d), lambda i: (0, 0))],
        out_specs=pl.BlockSpec((tm, d), lambda i: (i, 0)),
        compiler_params=_params(("parallel",)),
        name="final_norm",
    )(x2d, g)


def _rotary_tables(seq):
    pos = jnp.arange(seq, dtype=F32)
    inv_freq = ROPE_THETA ** (-jnp.arange(0, ROPE_DIM, 2, dtype=F32) / ROPE_DIM)
    ang = pos[:, None] * inv_freq[None, :]
    cos, sin = jnp.cos(ang), jnp.sin(ang)
    half = ROPE_DIM // 2
    ones = jnp.ones((seq, LANES - ROPE_DIM), F32)
    tab_c = jnp.concatenate([cos, cos, ones], axis=-1)
    tab_s1 = jnp.concatenate([-sin, jnp.zeros((seq, LANES - half), F32)], axis=-1)
    tab_s2 = jnp.concatenate([jnp.zeros((seq, half), F32), sin,
                              jnp.zeros((seq, LANES - ROPE_DIM), F32)], axis=-1)
    return tab_c, tab_s1, tab_s2


def _pad_lanes(v):
    return jnp.pad(v, (0, LANES - v.shape[0])).reshape(1, LANES)


def kernel(x, c, norm1, norm2, w_ada, b_ada, w_in, w_out, sgu_ln_g, sgu_ln_b, sgu_w, sgu_b, conv_w, conv_b, dt_bias, a_log, d_skip, ssm_norm, w_coarse, b_coarse, w_fine, b_fine, w_gate, w_up, w_down, final_norm):
    bsz, seq, d = x.shape
    depth = w_in.shape[0]
    n_tok = bsz * seq

    c_pad = jnp.pad(c, ((0, 8 - bsz % 8 if bsz % 8 else 0), (0, 0)))
    mod = _ada_mod(c_pad, w_ada, b_ada)[:, :bsz].reshape(depth, bsz, 6, 1, d)
    tab_c, tab_s1, tab_s2 = _rotary_tables(seq)

    for l in range(depth):
        sh1, sc1, g1, sh2, sc2, g2 = [mod[l, :, i] for i in range(6)]
        wl = w_in[l]
        w_main = jnp.concatenate(
            [wl[:, 3584:5120], wl[:, 0:512], wl[:, 2560:3584], wl[:, 512:1024], wl[:, 1024:2560]],
            axis=1).astype(BF16)
        w_dt = jnp.pad(wl[:, 5120:], ((0, 0), (0, LANES - SSM_HEADS))).astype(BF16)
        proj, dt_raw = _in_proj(x, norm1[l].reshape(1, d), sc1, sh1, w_main, w_dt)

        ya = _sgu(proj, sgu_ln_g[l].reshape(1, -1), sgu_ln_b[l].reshape(1, -1), sgu_w[l], sgu_b[l].T)
        qt, kr, vt, km = _rope_prep(proj, tab_c, tab_s1, tab_s2)
        yb = _moba(qt, kr, vt, km.reshape(km.shape[:3] + (ATT_HEAD_DIM,)))
        yc = _ssd(proj, dt_raw, conv_w[l], conv_b[l].reshape(1, -1), _pad_lanes(dt_bias[l]),
                  _pad_lanes(a_log[l]), jnp.repeat(d_skip[l], SSM_HEAD_DIM).reshape(1, -1),
                  ssm_norm[l].reshape(1, -1))

        wo = w_out[l].astype(BF16)
        w_r = jnp.concatenate([w_coarse[l], w_fine[l].transpose(1, 0, 2).reshape(d, MOE_EXPERTS)], axis=1)
        w_r = jnp.pad(w_r, ((0, 0), (0, LANES - w_r.shape[1])))
        b_r = _pad_lanes(jnp.concatenate([b_coarse[l], b_fine[l].reshape(-1)]))
        x, h_chunks, route = _out_proj_route(
            ya, yb, yc, x, g1, wo[:SGU_WIDTH], wo[SGU_WIDTH:2 * SGU_WIDTH], wo[2 * SGU_WIDTH:],
            norm2[l].reshape(1, d), sc2, sh2, w_r, b_r)

        route2d = route.reshape(n_tok, LANES)
        dest, blk_exp, n_blocks = _dispatch_plan(route2d)
        xs = _dispatch(dest, h_chunks, n_blocks)
        y_rows = _experts(blk_exp, xs, w_gate[l], w_up[l], w_down[l])
        x = _combine(dest, y_rows, x.reshape(n_tok, d), g2, route2d, seq).reshape(bsz, seq, d)

    return _final_norm(x.reshape(n_tok, d), final_norm.reshape(1, d)).reshape(bsz, seq, d)
```

```python
import functools

import jax
import jax.numpy as jnp
from jax import lax
from jax.experimental import pallas as pl
from jax.experimental.pallas import tpu as pltpu

F32 = jnp.float32
BF16 = jnp.bfloat16

LANES = 128
NORM_EPS = 1e-6
NEG_INF = -1e30

D_MODEL = 2048
N_CHUNK = D_MODEL // LANES
SGU_WIDTH = 512
SGU_GROUPS = 4
SGU_CHUNK = 128
ATT_HEADS = 4
ATT_HEAD_DIM = 128
ROPE_DIM = 32
ROPE_THETA = 500000.0
MOBA_BLOCK = 256
MOBA_TOPK = 3
SSM_WIDTH = 1024
SSM_HEAD_DIM = 64
SSM_HEADS = 16
SSM_GROUPS = 2
SSM_STATE = 128
SSM_CONV = 4
SSM_CHUNK = 128
SSM_CONV_DIM = 1536
MOE_GROUPS = 4
MOE_EPG = 8
MOE_EXPERTS = 32
MOE_FF = 512
MOE_BLOCK = 128

PROJ_MAIN = 5120
OFF_U, OFF_Z, OFF_V, OFF_Q, OFF_K, OFF_VA = 1536, 2048, 3072, 3584, 4096, 4608

VMEM_LIMIT = 56 * 1024 * 1024


def _params(sem, vmem=VMEM_LIMIT):
    return pltpu.CompilerParams(dimension_semantics=sem, vmem_limit_bytes=vmem)


def _silu(x):
    return x * jax.nn.sigmoid(x)


def _ada_kernel(c_ref, w_ref, b_ref, o_ref):
    ca = _silu(c_ref[...]).astype(BF16)
    o_ref[0] = jnp.dot(ca, w_ref[0].astype(BF16), preferred_element_type=F32) + b_ref[0]


def _ada_mod(c_pad, w_ada, b_ada):
    depth, d, n = w_ada.shape
    tn = 1024
    return pl.pallas_call(
        _ada_kernel,
        out_shape=jax.ShapeDtypeStruct((depth, c_pad.shape[0], n), F32),
        grid=(depth, n // tn),
        in_specs=[
            pl.BlockSpec(c_pad.shape, lambda l, j: (0, 0)),
            pl.BlockSpec((1, d, tn), lambda l, j: (l, 0, j)),
            pl.BlockSpec((1, 1, tn), lambda l, j: (l, 0, j)),
        ],
        out_specs=pl.BlockSpec((1, c_pad.shape[0], tn), lambda l, j: (l, 0, j)),
        compiler_params=_params(("parallel", "parallel")),
        name="ada_mod",
    )(c_pad, w_ada, b_ada.reshape(depth, 1, n))


def _modulated_norm(x, g, sc, sh):
    ms = jnp.mean(x * x, axis=-1, keepdims=True)
    return (x * lax.rsqrt(ms + NORM_EPS)) * g * (1.0 + sc) + sh


def _inproj_kernel(x_ref, g_ref, sc_ref, sh_ref, w_ref, wdt_ref, o_ref, dt_ref, h_ref):
    @pl.when(pl.program_id(2) == 0)
    def _():
        h = _modulated_norm(x_ref[0], g_ref[...], sc_ref[0], sh_ref[0]).astype(BF16)
        h_ref[...] = h
        dt_ref[0] = jnp.dot(h, wdt_ref[...], preferred_element_type=F32)

    o_ref[0] = jnp.dot(h_ref[...], w_ref[...], preferred_element_type=F32)


def _in_proj(x, g, sc, sh, w, wdt):
    bsz, seq, d = x.shape
    n = w.shape[1]
    tm, tn = 512, 1024
    return pl.pallas_call(
        _inproj_kernel,
        out_shape=(jax.ShapeDtypeStruct((bsz, seq, n), F32),
                   jax.ShapeDtypeStruct((bsz, seq, LANES), F32)),
        grid=(bsz, seq // tm, n // tn),
        in_specs=[
            pl.BlockSpec((1, tm, d), lambda b, i, j: (b, i, 0)),
            pl.BlockSpec((1, d), lambda b, i, j: (0, 0)),
            pl.BlockSpec((1, 1, d), lambda b, i, j: (b, 0, 0)),
            pl.BlockSpec((1, 1, d), lambda b, i, j: (b, 0, 0)),
            pl.BlockSpec((d, tn), lambda b, i, j: (0, j)),
            pl.BlockSpec((d, LANES), lambda b, i, j: (0, 0)),
        ],
        out_specs=(pl.BlockSpec((1, tm, tn), lambda b, i, j: (b, i, j)),
                   pl.BlockSpec((1, tm, LANES), lambda b, i, j: (b, i, 0))),
        scratch_shapes=[pltpu.VMEM((tm, d), BF16)],
        compiler_params=_params(("parallel", "parallel", "arbitrary")),
        name="in_proj",
    )(x, g, sc, sh, w, wdt)


def _sgu_kernel(u_ref, v_ref, g_ref, b_ref, w_ref, bs_ref, o_ref, *, n_chunks):
    row = lax.broadcasted_iota(jnp.int32, (SGU_CHUNK, SGU_CHUNK), 0)
    col = lax.broadcasted_iota(jnp.int32, (SGU_CHUNK, SGU_CHUNK), 1)
    w_causal = [jnp.where(row >= col, w_ref[g], 0.0).astype(BF16) for g in range(SGU_GROUPS)]
    for ci in range(n_chunks):
        rows = slice(ci * SGU_CHUNK, (ci + 1) * SGU_CHUNK)
        u = jax.nn.gelu(u_ref[0, rows, :])
        vf = jax.nn.gelu(v_ref[0, rows, :])
        mu = jnp.mean(vf, axis=-1, keepdims=True)
        var = jnp.mean(jnp.square(vf - mu), axis=-1, keepdims=True)
        vn = ((vf - mu) * lax.rsqrt(var + NORM_EPS) * g_ref[...] + b_ref[...]).astype(BF16)
        for g in range(SGU_GROUPS):
            cols = slice(g * LANES, (g + 1) * LANES)
            s = jnp.dot(w_causal[g], vn[:, cols], preferred_element_type=F32) + bs_ref[:, g:g + 1]
            o_ref[0, rows, cols] = (u[:, cols] * s).astype(BF16)


def _sgu(proj, ln_g, ln_b, w_s, b_s_t):
    bsz, seq, _ = proj.shape
    tc = 512
    blk = lambda off: pl.BlockSpec((1, tc, SGU_WIDTH), lambda b, i: (b, i, off // SGU_WIDTH))
    return pl.pallas_call(
        functools.partial(_sgu_kernel, n_chunks=tc // SGU_CHUNK),
        out_shape=jax.ShapeDtypeStruct((bsz, seq, SGU_WIDTH), BF16),
        grid=(bsz, seq // tc),
        in_specs=[
            blk(OFF_U), blk(OFF_V),
            pl.BlockSpec((1, SGU_WIDTH), lambda b, i: (0, 0)),
            pl.BlockSpec((1, SGU_WIDTH), lambda b, i: (0, 0)),
            pl.BlockSpec((SGU_GROUPS, SGU_CHUNK, SGU_CHUNK), lambda b, i: (0, 0, 0)),
            pl.BlockSpec((SGU_CHUNK, SGU_GROUPS), lambda b, i: (0, 0)),
        ],
        out_specs=pl.BlockSpec((1, tc, SGU_WIDTH), lambda b, i: (b, i, 0)),
        compiler_params=_params(("parallel", "parallel")),
        name="sgu",
    )(proj, proj, ln_g, ln_b, w_s, b_s_t)


def _rotary(t, c, s1, s2):
    half = ROPE_DIM // 2
    return t * c + pltpu.roll(t, LANES - half, 1) * s1 + pltpu.roll(t, half, 1) * s2


def _rope_kernel(q_ref, k_ref, v_ref, c_ref, s1_ref, s2_ref, qt_ref, kr_ref, vt_ref, km_ref):
    c, s1, s2 = c_ref[...], s1_ref[...], s2_ref[...]
    q = _rotary(q_ref[0], c, s1, s2)
    k = _rotary(k_ref[0], c, s1, s2)
    qt_ref[0, 0, 0] = q.T
    kr_ref[0, 0, 0] = k.astype(BF16)
    vt_ref[0, 0, 0] = v_ref[0].T.astype(BF16)
    km_ref[0, 0, 0] = jnp.mean(k, axis=0, keepdims=True)


def _rope_prep(proj, tab_c, tab_s1, tab_s2):
    bsz, seq, _ = proj.shape
    nblk = seq // MOBA_BLOCK
    hd = ATT_HEAD_DIM
    sec = lambda off: pl.BlockSpec((1, MOBA_BLOCK, hd), lambda b, h, i: (b, i, off // hd + h))
    tab = pl.BlockSpec((MOBA_BLOCK, hd), lambda b, h, i: (i, 0))
    return pl.pallas_call(
        _rope_kernel,
        out_shape=(
            jax.ShapeDtypeStruct((bsz, ATT_HEADS, nblk, hd, MOBA_BLOCK), F32),
            jax.ShapeDtypeStruct((bsz, ATT_HEADS, nblk, MOBA_BLOCK, hd), BF16),
            jax.ShapeDtypeStruct((bsz, ATT_HEADS, nblk, hd, MOBA_BLOCK), BF16),
            jax.ShapeDtypeStruct((bsz, ATT_HEADS, nblk, 1, hd), F32),
        ),
        grid=(bsz, ATT_HEADS, nblk),
        in_specs=[sec(OFF_Q), sec(OFF_K), sec(OFF_VA), tab, tab, tab],
        out_specs=(
            pl.BlockSpec((1, 1, 1, hd, MOBA_BLOCK), lambda b, h, i: (b, h, i, 0, 0)),
            pl.BlockSpec((1, 1, 1, MOBA_BLOCK, hd), lambda b, h, i: (b, h, i, 0, 0)),
            pl.BlockSpec((1, 1, 1, hd, MOBA_BLOCK), lambda b, h, i: (b, h, i, 0, 0)),
            pl.BlockSpec((1, 1, 1, 1, hd), lambda b, h, i: (b, h, i, 0, 0)),
        ),
        compiler_params=_params(("parallel", "parallel", "parallel")),
        name="rope_prep",
    )(proj, proj, proj, tab_c, tab_s1, tab_s2)


def _moba_kernel(qt_ref, kr_ref, vt_ref, km_ref, o_ref, sel_ref, *, nblk):
    i = pl.program_id(2)
    tq = MOBA_BLOCK
    scale = ATT_HEAD_DIM ** -0.5
    qt = qt_ref[0, 0, 0]
    qtb = qt.astype(BF16)

    gate = jnp.dot(km_ref[0, 0], qt, preferred_element_type=F32,
                   precision=lax.Precision.HIGHEST)
    blk = lax.broadcasted_iota(jnp.int32, (nblk, tq), 0)
    valid = blk < i
    gate = jnp.where(valid, gate, NEG_INF)
    cnt = jnp.zeros((nblk, tq), F32)
    for jp in range(nblk):
        gj = gate[jp:jp + 1, :]
        beats = (gj > gate) | ((gj == gate) & (blk > jp))
        cnt = cnt + jnp.where(beats, 1.0, 0.0)
    sel_ref[...] = jnp.where(valid & (cnt < MOBA_TOPK), 1.0, 0.0)

    krow = lax.broadcasted_iota(jnp.int32, (tq, tq), 0)
    qcol = lax.broadcasted_iota(jnp.int32, (tq, tq), 1)
    s = jnp.dot(kr_ref[0, 0, i], qtb, preferred_element_type=F32) * scale
    s = jnp.where(krow <= qcol, s, NEG_INF)
    m0 = jnp.max(s, axis=0, keepdims=True)
    p = jnp.exp(s - m0)
    l0 = jnp.sum(p, axis=0, keepdims=True)
    acc0 = jnp.dot(vt_ref[0, 0, i], p.astype(BF16), preferred_element_type=F32)

    def body(j, carry):
        m, l, acc = carry
        s = jnp.dot(kr_ref[0, 0, j], qtb, preferred_element_type=F32) * scale
        s = jnp.where(sel_ref[pl.ds(j, 1), :] > 0.5, s, NEG_INF)
        m_new = jnp.maximum(m, jnp.max(s, axis=0, keepdims=True))
        alpha = jnp.exp(m - m_new)
        p = jnp.exp(s - m_new)
        l = alpha * l + jnp.sum(p, axis=0, keepdims=True)
        acc = alpha * acc + jnp.dot(vt_ref[0, 0, j], p.astype(BF16), preferred_element_type=F32)
        return m_new, l, acc

    _, l, acc = lax.fori_loop(0, i, body, (m0, l0, acc0))
    o_ref[0] = (acc / l).T.astype(BF16)


def _moba(qt, kr, vt, km):
    bsz, nh, nblk, hd, tq = qt.shape
    seq = nblk * tq
    full = lambda shape: pl.BlockSpec((1, 1) + shape, lambda b, h, i: (b, h) + (0,) * len(shape))
    return pl.pallas_call(
        functools.partial(_moba_kernel, nblk=nblk),
        out_shape=jax.ShapeDtypeStruct((bsz, seq, nh * hd), BF16),
        grid=(bsz, nh, nblk),
        in_specs=[
            pl.BlockSpec((1, 1, 1, hd, tq), lambda b, h, i: (b, h, i, 0, 0)),
            full((nblk, tq, hd)), full((nblk, hd, tq)), full((nblk, hd)),
        ],
        out_specs=pl.BlockSpec((1, tq, hd), lambda b, h, i: (b, i, h)),
        scratch_shapes=[pltpu.VMEM((nblk, tq), F32)],
        compiler_params=_params(("parallel", "parallel", "arbitrary")),
        name="moba",
    )(qt, kr, vt, km)


def _softplus(x):
    return jnp.maximum(x, 0.0) + jnp.log1p(jnp.exp(-jnp.abs(x)))


def _ssd_kernel(xbc_ref, z_ref, dt_ref, cw_ref, cb_ref, dtb_ref, alog_ref, dsk_ref, nw_ref,
                o_ref, prev_ref, state_ref, y_ref):
    L, P, N = SSM_CHUNK, SSM_HEAD_DIM, SSM_STATE
    hi = lax.Precision.HIGHEST

    @pl.when(pl.program_id(1) == 0)
    def _():
        prev_ref[...] = jnp.zeros_like(prev_ref)
        state_ref[...] = jnp.zeros_like(state_ref)

    cur = xbc_ref[0]
    prev = prev_ref[...]
    crow = lax.broadcasted_iota(jnp.int32, cur.shape, 0)
    acc = cur * cw_ref[SSM_CONV - 1:SSM_CONV, :] + cb_ref[...]
    for j in range(1, SSM_CONV):
        shifted = pltpu.roll(jnp.where(crow >= L - j, prev, cur), j, 0)
        acc = acc + shifted * cw_ref[SSM_CONV - 1 - j:SSM_CONV - j, :]
    prev_ref[...] = cur
    xc = _silu(acc)
    xs = xc[:, :SSM_WIDTH]
    bm = xc[:, SSM_WIDTH:SSM_WIDTH + SSM_GROUPS * N].astype(BF16)
    cm = xc[:, SSM_WIDTH + SSM_GROUPS * N:].astype(BF16)
    xs_b = xs.astype(BF16)
    xs_t = xs.T

    dt = _softplus(dt_ref[0] + dtb_ref[...])
    da = dt * (-jnp.exp(alog_ref[...]))
    row = lax.broadcasted_iota(jnp.int32, (L, L), 0)
    col = lax.broadcasted_iota(jnp.int32, (L, L), 1)
    causal = row >= col
    acum = jnp.dot(jnp.where(causal, 1.0, 0.0), da, preferred_element_type=F32, precision=hi)
    acum_t = jnp.dot(da.T, jnp.where(row <= col, 1.0, 0.0), preferred_element_type=F32, precision=hi)
    dt_t = dt.T
    last_t = acum_t[:, L - 1:L]
    w_end_t = jnp.exp(last_t - acum_t) * dt_t
    e_last_t = jnp.exp(last_t)
    e_acum = jnp.exp(acum)

    for g in range(SSM_GROUPS):
        b_g = bm[:, g * N:(g + 1) * N]
        c_g = cm[:, g * N:(g + 1) * N]
        cb = lax.dot_general(c_g, b_g, (((1,), (1,)), ((), ())), preferred_element_type=F32)
        for hh in range(SSM_HEADS // SSM_GROUPS):
            h = g * (SSM_HEADS // SSM_GROUPS) + hh
            hs = slice(h * P, (h + 1) * P)
            decay = jnp.exp(jnp.where(causal, acum[:, h:h + 1] - acum_t[h:h + 1, :], -jnp.inf))
            mat = (cb * decay * dt_t[h:h + 1, :]).astype(BF16)
            y = jnp.dot(mat, xs_b[:, hs], preferred_element_type=F32)
            st = state_ref[hs, :]
            y = y + lax.dot_general(c_g, st.astype(BF16), (((1,), (1,)), ((), ())),
                                    preferred_element_type=F32) * e_acum[:, h:h + 1]
            y_ref[:, hs] = y
            xw = (xs_t[hs, :] * w_end_t[h:h + 1, :]).astype(BF16)
            state_ref[hs, :] = st * e_last_t[h:h + 1, :] + jnp.dot(xw, b_g, preferred_element_type=F32)

    y = y_ref[...] + dsk_ref[...] * xs
    y = y * _silu(z_ref[0])
    gw = SSM_WIDTH // SSM_GROUPS
    for g in range(SSM_GROUPS):
        cols = slice(g * gw, (g + 1) * gw)
        yg = y[:, cols]
        yg = yg * lax.rsqrt(jnp.mean(yg * yg, axis=-1, keepdims=True) + NORM_EPS)
        o_ref[0, :, cols] = (yg * nw_ref[:, cols]).astype(BF16)


def _ssd(proj, dt_raw, conv_w, conv_b, dt_bias, a_log, d_skip, ssm_norm):
    bsz, seq, _ = proj.shape
    L = SSM_CHUNK
    vec = lambda n: pl.BlockSpec((1, n), lambda b, c: (0, 0))
    return pl.pallas_call(
        _ssd_kernel,
        out_shape=jax.ShapeDtypeStruct((bsz, seq, SSM_WIDTH), BF16),
        grid=(bsz, seq // L),
        in_specs=[
            pl.BlockSpec((1, L, SSM_CONV_DIM), lambda b, c: (b, c, 0)),
            pl.BlockSpec((1, L, SSM_WIDTH), lambda b, c: (b, c, OFF_Z // SSM_WIDTH)),
            pl.BlockSpec((1, L, LANES), lambda b, c: (b, c, 0)),
            pl.BlockSpec((SSM_CONV, SSM_CONV_DIM), lambda b, c: (0, 0)),
            vec(SSM_CONV_DIM), vec(LANES), vec(LANES), vec(SSM_WIDTH), vec(SSM_WIDTH),
        ],
        out_specs=pl.BlockSpec((1, L, SSM_WIDTH), lambda b, c: (b, c, 0)),
        scratch_shapes=[
            pltpu.VMEM((L, SSM_CONV_DIM), F32),
            pltpu.VMEM((SSM_WIDTH, SSM_STATE), F32),
            pltpu.VMEM((L, SSM_WIDTH), F32),
        ],
        compiler_params=_params(("parallel", "arbitrary")),
        name="ssd",
    )(proj, proj, dt_raw, conv_w, conv_b, dt_bias, a_log, d_skip, ssm_norm)


def _route(logits):
    lane = lax.broadcasted_iota(jnp.int32, logits.shape, 1).astype(F32)
    big = float(LANES)

    def first_argmax(vals, mask):
        top = jnp.max(vals, axis=-1, keepdims=True)
        idx = jnp.min(jnp.where(mask & (vals >= top), lane, big), axis=-1, keepdims=True)
        return top, idx

    def masked_softmax(mask):
        z = jnp.where(mask, logits, -jnp.inf)
        e = jnp.exp(z - jnp.max(z, axis=-1, keepdims=True))
        return e / jnp.sum(e, axis=-1, keepdims=True)

    gmask = lane < MOE_GROUPS
    g_prob, g_idx = first_argmax(jnp.where(gmask, masked_softmax(gmask), -1.0), gmask)
    lo = MOE_GROUPS + g_idx * MOE_EPG
    fmask = (lane >= lo) & (lane < lo + MOE_EPG)
    p_exp = jnp.where(fmask, masked_softmax(fmask), -1.0)
    p1, i1 = first_argmax(p_exp, fmask)
    rest = fmask & (lane != i1)
    p2, i2 = first_argmax(jnp.where(rest, p_exp, -1.0), rest)
    den = p1 + p2
    out = jnp.where(lane == 0, i1 - MOE_GROUPS, 0.0)
    out = jnp.where(lane == 1, i2 - MOE_GROUPS, out)
    out = jnp.where(lane == 2, g_prob * p1 / den, out)
    out = jnp.where(lane == 3, g_prob * p2 / den, out)
    return out


def _outproj_kernel(ya_ref, yb_ref, yc_ref, x_ref, g1_ref, wa_ref, wb_ref, wc_ref,
                    n2_ref, sc_ref, sh_ref, wr_ref, br_ref, xo_ref, hc_ref, rt_ref):
    mix = jnp.dot(ya_ref[0], wa_ref[...], preferred_element_type=F32)
    mix = mix + jnp.dot(yb_ref[0], wb_ref[...], preferred_element_type=F32)
    mix = mix + jnp.dot(yc_ref[0], wc_ref[...], preferred_element_type=F32)
    xn = x_ref[0] + g1_ref[0] * mix
    xo_ref[0] = xn
    h = _modulated_norm(xn, n2_ref[...], sc_ref[0], sh_ref[0])
    for c in range(N_CHUNK):
        hc_ref[0, c] = h[:, c * LANES:(c + 1) * LANES]
    logits = jnp.dot(h, wr_ref[...], preferred_element_type=F32,
                     precision=lax.Precision.HIGHEST) + br_ref[...]
    rt_ref[0] = _route(logits)


TM_TOK = 256


def _out_proj_route(ya, yb, yc, x, g1, wa, wb, wc, n2, sc2, sh2, w_r, b_r):
    bsz, seq, d = x.shape
    tm = TM_TOK
    nt = seq // tm
    row = lambda w: pl.BlockSpec((1, tm, w), lambda b, i: (b, i, 0))
    per_b = pl.BlockSpec((1, 1, d), lambda b, i: (b, 0, 0))
    const = lambda a: pl.BlockSpec(a.shape, lambda b, i: (0,) * a.ndim)
    return pl.pallas_call(
        _outproj_kernel,
        out_shape=(jax.ShapeDtypeStruct((bsz, seq, d), F32),
                   jax.ShapeDtypeStruct((bsz * nt, N_CHUNK, tm, LANES), F32),
                   jax.ShapeDtypeStruct((bsz, seq, LANES), F32)),
        grid=(bsz, nt),
        in_specs=[row(ya.shape[-1]), row(yb.shape[-1]), row(yc.shape[-1]), row(d), per_b,
                  const(wa), const(wb), const(wc), const(n2), per_b, per_b, const(w_r), const(b_r)],
        out_specs=(row(d),
                   pl.BlockSpec((1, N_CHUNK, tm, LANES), lambda b, i: (b * nt + i, 0, 0, 0)),
                   row(LANES)),
        compiler_params=_params(("parallel", "parallel")),
        name="out_proj_route",
    )(ya, yb, yc, x, g1, wa, wb, wc, n2, sc2, sh2, w_r, b_r)


BLOCK_SHIFT = MOE_BLOCK.bit_length() - 1


TOK_SHIFT = TM_TOK.bit_length() - 1
GATHER_UNROLL = 8


def _expert_kernel(be_ref, tok_ref, h_hbm, wg_ref, wu_ref, wd_ref, o_ref,
                   buf_ref, wgb_ref, wub_ref, wdb_ref, sem):
    i = pl.program_id(0)
    slot = i % 2

    def gather(blk, slot_, mi):
        tok = tok_ref[blk * MOE_BLOCK + mi]
        return pltpu.make_async_copy(
            h_hbm.at[jnp.right_shift(tok, TOK_SHIFT), :, jnp.bitwise_and(tok, TM_TOK - 1), :],
            buf_ref.at[slot_, :, mi, :], sem.at[slot_])

    def for_rows(fn):
        def body(r, carry):
            for u in range(GATHER_UNROLL):
                fn(r * GATHER_UNROLL + u)
            return carry
        lax.fori_loop(0, MOE_BLOCK // GATHER_UNROLL, body, 0)

    @pl.when(i == 0)
    def _():
        for_rows(lambda mi: gather(0, 0, mi).start())

    @pl.when(i + 1 < pl.num_programs(0))
    def _():
        for_rows(lambda mi: gather(i + 1, 1 - slot, mi).start())

    @pl.when((i == 0) | (be_ref[i] != be_ref[jnp.maximum(i - 1, 0)]))
    def _():
        wgb_ref[...] = wg_ref[0, 0].astype(BF16)
        wub_ref[...] = wu_ref[0, 0].astype(BF16)
        wdb_ref[...] = wd_ref[0, 0].astype(BF16)

    for_rows(lambda mi: gather(i, slot, mi).wait())
    x = jnp.concatenate([buf_ref[slot, c] for c in range(N_CHUNK)], axis=-1).astype(BF16)
    gate = jnp.dot(x, wgb_ref[...], preferred_element_type=F32)
    up = jnp.dot(x, wub_ref[...], preferred_element_type=F32)
    hid = (_silu(gate) * up).astype(BF16)
    y = jnp.dot(hid, wdb_ref[...], preferred_element_type=F32)
    for c in range(N_CHUNK):
        o_ref[0, c] = y[:, c * LANES:(c + 1) * LANES]


def _experts(layer, blk_exp, row_tok, h_chunks, w_gate, w_up, w_down):
    n_blocks = blk_exp.shape[0]
    d, ff = w_gate.shape[2], w_gate.shape[3]
    return pl.pallas_call(
        _expert_kernel,
        out_shape=jax.ShapeDtypeStruct((n_blocks, N_CHUNK, MOE_BLOCK, LANES), F32),
        grid_spec=pltpu.PrefetchScalarGridSpec(
            num_scalar_prefetch=2,
            grid=(n_blocks,),
            in_specs=[pl.BlockSpec(memory_space=pl.ANY),
                      pl.BlockSpec((1, 1, d, ff), lambda i, be, rt: (layer, be[i], 0, 0)),
                      pl.BlockSpec((1, 1, d, ff), lambda i, be, rt: (layer, be[i], 0, 0)),
                      pl.BlockSpec((1, 1, ff, d), lambda i, be, rt: (layer, be[i], 0, 0))],
            out_specs=pl.BlockSpec((1, N_CHUNK, MOE_BLOCK, LANES), lambda i, be, rt: (i, 0, 0, 0)),
            scratch_shapes=[pltpu.VMEM((2, N_CHUNK, MOE_BLOCK, LANES), F32),
                            pltpu.VMEM((d, ff), BF16), pltpu.VMEM((d, ff), BF16),
                            pltpu.VMEM((ff, d), BF16), pltpu.SemaphoreType.DMA((2,))],
        ),
        compiler_params=_params(("arbitrary",)),
        name="moe_experts",
    )(blk_exp, row_tok, h_chunks, w_gate, w_up, w_down)


def _combine_kernel(dest_ref, y_hbm, x_ref, g2_ref, rt_ref, o_ref, buf_ref, sem):
    i = pl.program_id(0)
    tm = TM_TOK

    def gather(k, mi):
        d = dest_ref[(i * tm + mi) * 2 + k]
        return pltpu.make_async_copy(
            y_hbm.at[jnp.right_shift(d, BLOCK_SHIFT), :, jnp.bitwise_and(d, MOE_BLOCK - 1), :],
            buf_ref.at[k, :, mi, :], sem)

    def issue(mi, carry):
        for k in range(2):
            gather(k, mi).start()
        return carry

    def drain(mi, carry):
        for k in range(2):
            gather(k, mi).wait()
        return carry

    lax.fori_loop(0, tm, issue, 0)
    lax.fori_loop(0, tm, drain, 0)
    rt = rt_ref[...]
    gate0, gate1 = rt[:, 2:3], rt[:, 3:4]
    for c in range(N_CHUNK):
        cols = slice(c * LANES, (c + 1) * LANES)
        moe = buf_ref[0, c] * gate0 + buf_ref[1, c] * gate1
        o_ref[:, cols] = x_ref[:, cols] + g2_ref[0, :, cols] * moe


def _combine(dest_flat, y_rows, x2d, g2, route2d, seq):
    n_tok, d = x2d.shape
    tm = TM_TOK
    per_seq = seq // tm
    return pl.pallas_call(
        _combine_kernel,
        out_shape=jax.ShapeDtypeStruct((n_tok, d), F32),
        grid_spec=pltpu.PrefetchScalarGridSpec(
            num_scalar_prefetch=1,
            grid=(n_tok // tm,),
            in_specs=[pl.BlockSpec(memory_space=pl.ANY),
                      pl.BlockSpec((tm, d), lambda i, dr: (i, 0)),
                      pl.BlockSpec((1, 1, d), lambda i, dr: (i // per_seq, 0, 0)),
                      pl.BlockSpec((tm, LANES), lambda i, dr: (i, 0))],
            out_specs=pl.BlockSpec((tm, d), lambda i, dr: (i, 0)),
            scratch_shapes=[pltpu.VMEM((2, N_CHUNK, tm, LANES), F32), pltpu.SemaphoreType.DMA],
        ),
        compiler_params=_params(("arbitrary",)),
        name="moe_combine",
    )(dest_flat, y_rows, x2d, g2, route2d)


def _dispatch_plan(route2d):
    n_tok = route2d.shape[0]
    n_asg = n_tok * 2
    flat_e = route2d[:, 0:2].astype(jnp.int32).reshape(n_asg)
    asg = jnp.arange(n_asg, dtype=jnp.int32)
    s_e, order = lax.sort((flat_e, asg), num_keys=1)
    ids = jnp.arange(MOE_EXPERTS, dtype=jnp.int32)
    start = jnp.sum(s_e[:, None] < ids[None, :], axis=0, dtype=jnp.int32)
    counts = jnp.sum(s_e[:, None] == ids[None, :], axis=0, dtype=jnp.int32)
    pcounts = (counts + MOE_BLOCK - 1) // MOE_BLOCK * MOE_BLOCK
    pend = jnp.cumsum(pcounts)
    pstart = pend - pcounts
    dest_sorted = pstart[s_e] + asg - start[s_e]
    _, dest = lax.sort((order, dest_sorted), num_keys=1)
    n_blocks = n_asg // MOE_BLOCK + MOE_EXPERTS
    blk_start = jnp.arange(n_blocks, dtype=jnp.int32) * MOE_BLOCK
    blk_exp = jnp.minimum(jnp.sum(pend[None, :] <= blk_start[:, None], axis=1, dtype=jnp.int32),
                          MOE_EXPERTS - 1)
    row = jnp.arange(n_blocks * MOE_BLOCK, dtype=jnp.int32)
    row_e = jnp.repeat(blk_exp, MOE_BLOCK)
    off = row - pstart[row_e]
    src = jnp.minimum(start[row_e] + off, n_asg - 1)
    row_tok = jnp.where(off < counts[row_e], order[src] // 2, 0).astype(jnp.int32)
    return dest, row_tok, blk_exp


def _final_norm_kernel(x_ref, g_ref, o_ref):
    x = x_ref[...]
    ms = jnp.mean(x * x, axis=-1, keepdims=True)
    o_ref[...] = (x * lax.rsqrt(ms + NORM_EPS)) * g_ref[...]


def _final_norm(x2d, g):
    n_tok, d = x2d.shape
    tm = 512
    return pl.pallas_call(
        _final_norm_kernel,
        out_shape=jax.ShapeDtypeStruct((n_tok, d), F32),
        grid=(n_tok // tm,),
        in_specs=[pl.BlockSpec((tm, d), lambda i: (i, 0)), pl.BlockSpec((1, d), lambda i: (0, 0))],
        out_specs=pl.BlockSpec((tm, d), lambda i: (i, 0)),
        compiler_params=_params(("parallel",)),
        name="final_norm",
    )(x2d, g)


def _rotary_tables(seq):
    pos = jnp.arange(seq, dtype=F32)
    inv_freq = ROPE_THETA ** (-jnp.arange(0, ROPE_DIM, 2, dtype=F32) / ROPE_DIM)
    ang = pos[:, None] * inv_freq[None, :]
    cos, sin = jnp.cos(ang), jnp.sin(ang)
    half = ROPE_DIM // 2
    ones = jnp.ones((seq, LANES - ROPE_DIM), F32)
    tab_c = jnp.concatenate([cos, cos, ones], axis=-1)
    tab_s1 = jnp.concatenate([-sin, jnp.zeros((seq, LANES - half), F32)], axis=-1)
    tab_s2 = jnp.concatenate([jnp.zeros((seq, half), F32), sin,
                              jnp.zeros((seq, LANES - ROPE_DIM), F32)], axis=-1)
    return tab_c, tab_s1, tab_s2


def _pad_lanes(v):
    return jnp.pad(v, (0, LANES - v.shape[0])).reshape(1, LANES)


def kernel(x, c, norm1, norm2, w_ada, b_ada, w_in, w_out, sgu_ln_g, sgu_ln_b, sgu_w, sgu_b, conv_w, conv_b, dt_bias, a_log, d_skip, ssm_norm, w_coarse, b_coarse, w_fine, b_fine, w_gate, w_up, w_down, final_norm):
    bsz, seq, d = x.shape
    depth = w_in.shape[0]
    n_tok = bsz * seq

    c_pad = jnp.pad(c, ((0, 8 - bsz % 8 if bsz % 8 else 0), (0, 0)))
    mod = _ada_mod(c_pad, w_ada, b_ada)[:, :bsz].reshape(depth, bsz, 6, 1, d)
    tab_c, tab_s1, tab_s2 = _rotary_tables(seq)

    for l in range(depth):
        sh1, sc1, g1, sh2, sc2, g2 = [mod[l, :, i] for i in range(6)]
        wl = w_in[l]
        w_main = jnp.concatenate(
            [wl[:, 3584:5120], wl[:, 0:512], wl[:, 2560:3584], wl[:, 512:1024], wl[:, 1024:2560]],
            axis=1).astype(BF16)
        w_dt = jnp.pad(wl[:, 5120:], ((0, 0), (0, LANES - SSM_HEADS))).astype(BF16)
        proj, dt_raw = _in_proj(x, norm1[l].reshape(1, d), sc1, sh1, w_main, w_dt)

        ya = _sgu(proj, sgu_ln_g[l].reshape(1, -1), sgu_ln_b[l].reshape(1, -1), sgu_w[l], sgu_b[l].T)
        qt, kr, vt, km = _rope_prep(proj, tab_c, tab_s1, tab_s2)
        yb = _moba(qt, kr, vt, km.reshape(km.shape[:3] + (ATT_HEAD_DIM,)))
        yc = _ssd(proj, dt_raw, conv_w[l], conv_b[l].reshape(1, -1), _pad_lanes(dt_bias[l]),
                  _pad_lanes(a_log[l]), jnp.repeat(d_skip[l], SSM_HEAD_DIM).reshape(1, -1),
                  ssm_norm[l].reshape(1, -1))

        wo = w_out[l].astype(BF16)
        w_r = jnp.concatenate([w_coarse[l], w_fine[l].transpose(1, 0, 2).reshape(d, MOE_EXPERTS)], axis=1)
        w_r = jnp.pad(w_r, ((0, 0), (0, LANES - w_r.shape[1])))
        b_r = _pad_lanes(jnp.concatenate([b_coarse[l], b_fine[l].reshape(-1)]))
        x, h_chunks, route = _out_proj_route(
            ya, yb, yc, x, g1, wo[:SGU_WIDTH], wo[SGU_WIDTH:2 * SGU_WIDTH], wo[2 * SGU_WIDTH:],
            norm2[l].reshape(1, d), sc2, sh2, w_r, b_r)

        route2d = route.reshape(n_tok, LANES)
        dest, row_tok, blk_exp = _dispatch_plan(route2d)
        y_rows = _experts(l, blk_exp, row_tok, h_chunks, w_gate, w_up, w_down)
        x = _combine(dest, y_rows, x.reshape(n_tok, d), g2, route2d, seq).reshape(bsz, seq, d)

    return _final_norm(x.reshape(n_tok, d), final_norm.reshape(1, d)).reshape(bsz, seq, d)
```

```python
import functools

import jax
import jax.numpy as jnp
from jax import lax
from jax.experimental import pallas as pl
from jax.experimental.pallas import tpu as pltpu

F32 = jnp.float32
BF16 = jnp.bfloat16

LANES = 128
NORM_EPS = 1e-6
NEG_INF = -1e30

D_MODEL = 2048
N_CHUNK = D_MODEL // LANES
SGU_WIDTH = 512
SGU_GROUPS = 4
SGU_CHUNK = 128
ATT_HEADS = 4
ATT_HEAD_DIM = 128
ROPE_DIM = 32
ROPE_THETA = 500000.0
MOBA_BLOCK = 256
MOBA_TOPK = 3
SSM_WIDTH = 1024
SSM_HEAD_DIM = 64
SSM_HEADS = 16
SSM_GROUPS = 2
SSM_STATE = 128
SSM_CONV = 4
SSM_CHUNK = 128
SSM_CONV_DIM = 1536
MOE_GROUPS = 4
MOE_EPG = 8
MOE_EXPERTS = 32
MOE_FF = 512
MOE_BLOCK = 512

PROJ_MAIN = 5120
OFF_U, OFF_Z, OFF_V, OFF_Q, OFF_K, OFF_VA = 1536, 2048, 3072, 3584, 4096, 4608

VMEM_LIMIT = 56 * 1024 * 1024


def _params(sem, vmem=VMEM_LIMIT):
    return pltpu.CompilerParams(dimension_semantics=sem, vmem_limit_bytes=vmem)


def _silu(x):
    return x * jax.nn.sigmoid(x)


def _ada_kernel(c_ref, w_ref, b_ref, o_ref):
    ca = _silu(c_ref[...]).astype(BF16)
    o_ref[0] = jnp.dot(ca, w_ref[0].astype(BF16), preferred_element_type=F32) + b_ref[0]


def _ada_mod(c_pad, w_ada, b_ada):
    depth, d, n = w_ada.shape
    tn = 1024
    return pl.pallas_call(
        _ada_kernel,
        out_shape=jax.ShapeDtypeStruct((depth, c_pad.shape[0], n), F32),
        grid=(depth, n // tn),
        in_specs=[
            pl.BlockSpec(c_pad.shape, lambda l, j: (0, 0)),
            pl.BlockSpec((1, d, tn), lambda l, j: (l, 0, j)),
            pl.BlockSpec((1, 1, tn), lambda l, j: (l, 0, j)),
        ],
        out_specs=pl.BlockSpec((1, c_pad.shape[0], tn), lambda l, j: (l, 0, j)),
        compiler_params=_params(("parallel", "parallel")),
        name="ada_mod",
    )(c_pad, w_ada, b_ada.reshape(depth, 1, n))


def _modulated_norm(x, g, sc, sh):
    ms = jnp.mean(x * x, axis=-1, keepdims=True)
    return (x * lax.rsqrt(ms + NORM_EPS)) * g * (1.0 + sc) + sh


def _inproj_kernel(x_ref, g_ref, sc_ref, sh_ref, w_ref, wdt_ref, o_ref, dt_ref, h_ref):
    @pl.when(pl.program_id(2) == 0)
    def _():
        h = _modulated_norm(x_ref[0], g_ref[...], sc_ref[0], sh_ref[0]).astype(BF16)
        h_ref[...] = h
        dt_ref[0] = jnp.dot(h, wdt_ref[...], preferred_element_type=F32)

    o_ref[0] = jnp.dot(h_ref[...], w_ref[...], preferred_element_type=F32)


def _in_proj(x, g, sc, sh, w, wdt):
    bsz, seq, d = x.shape
    n = w.shape[1]
    tm, tn = 512, 1024
    return pl.pallas_call(
        _inproj_kernel,
        out_shape=(jax.ShapeDtypeStruct((bsz, seq, n), F32),
                   jax.ShapeDtypeStruct((bsz, seq, LANES), F32)),
        grid=(bsz, seq // tm, n // tn),
        in_specs=[
            pl.BlockSpec((1, tm, d), lambda b, i, j: (b, i, 0)),
            pl.BlockSpec((1, d), lambda b, i, j: (0, 0)),
            pl.BlockSpec((1, 1, d), lambda b, i, j: (b, 0, 0)),
            pl.BlockSpec((1, 1, d), lambda b, i, j: (b, 0, 0)),
            pl.BlockSpec((d, tn), lambda b, i, j: (0, j)),
            pl.BlockSpec((d, LANES), lambda b, i, j: (0, 0)),
        ],
        out_specs=(pl.BlockSpec((1, tm, tn), lambda b, i, j: (b, i, j)),
                   pl.BlockSpec((1, tm, LANES), lambda b, i, j: (b, i, 0))),
        scratch_shapes=[pltpu.VMEM((tm, d), BF16)],
        compiler_params=_params(("parallel", "parallel", "arbitrary")),
        name="in_proj",
    )(x, g, sc, sh, w, wdt)


def _sgu_kernel(u_ref, v_ref, g_ref, b_ref, w_ref, bs_ref, o_ref, *, n_chunks):
    row = lax.broadcasted_iota(jnp.int32, (SGU_CHUNK, SGU_CHUNK), 0)
    col = lax.broadcasted_iota(jnp.int32, (SGU_CHUNK, SGU_CHUNK), 1)
    w_causal = [jnp.where(row >= col, w_ref[g], 0.0).astype(BF16) for g in range(SGU_GROUPS)]
    for ci in range(n_chunks):
        rows = slice(ci * SGU_CHUNK, (ci + 1) * SGU_CHUNK)
        u = jax.nn.gelu(u_ref[0, rows, :])
        vf = jax.nn.gelu(v_ref[0, rows, :])
        mu = jnp.mean(vf, axis=-1, keepdims=True)
        var = jnp.mean(jnp.square(vf - mu), axis=-1, keepdims=True)
        vn = ((vf - mu) * lax.rsqrt(var + NORM_EPS) * g_ref[...] + b_ref[...]).astype(BF16)
        for g in range(SGU_GROUPS):
            cols = slice(g * LANES, (g + 1) * LANES)
            s = jnp.dot(w_causal[g], vn[:, cols], preferred_element_type=F32) + bs_ref[:, g:g + 1]
            o_ref[0, rows, cols] = (u[:, cols] * s).astype(BF16)


def _sgu(proj, ln_g, ln_b, w_s, b_s_t):
    bsz, seq, _ = proj.shape
    tc = 512
    blk = lambda off: pl.BlockSpec((1, tc, SGU_WIDTH), lambda b, i: (b, i, off // SGU_WIDTH))
    return pl.pallas_call(
        functools.partial(_sgu_kernel, n_chunks=tc // SGU_CHUNK),
        out_shape=jax.ShapeDtypeStruct((bsz, seq, SGU_WIDTH), BF16),
        grid=(bsz, seq // tc),
        in_specs=[
            blk(OFF_U), blk(OFF_V),
            pl.BlockSpec((1, SGU_WIDTH), lambda b, i: (0, 0)),
            pl.BlockSpec((1, SGU_WIDTH), lambda b, i: (0, 0)),
            pl.BlockSpec((SGU_GROUPS, SGU_CHUNK, SGU_CHUNK), lambda b, i: (0, 0, 0)),
            pl.BlockSpec((SGU_CHUNK, SGU_GROUPS), lambda b, i: (0, 0)),
        ],
        out_specs=pl.BlockSpec((1, tc, SGU_WIDTH), lambda b, i: (b, i, 0)),
        compiler_params=_params(("parallel", "parallel")),
        name="sgu",
    )(proj, proj, ln_g, ln_b, w_s, b_s_t)


def _rotary(t, c, s1, s2):
    half = ROPE_DIM // 2
    return t * c + pltpu.roll(t, LANES - half, 1) * s1 + pltpu.roll(t, half, 1) * s2


def _rope_kernel(q_ref, k_ref, v_ref, c_ref, s1_ref, s2_ref, qt_ref, kr_ref, vt_ref, km_ref):
    c, s1, s2 = c_ref[...], s1_ref[...], s2_ref[...]
    q = _rotary(q_ref[0], c, s1, s2)
    k = _rotary(k_ref[0], c, s1, s2)
    qt_ref[0, 0, 0] = q.T
    kr_ref[0, 0, 0] = k.astype(BF16)
    vt_ref[0, 0, 0] = v_ref[0].T.astype(BF16)
    km_ref[0, 0, 0] = jnp.mean(k, axis=0, keepdims=True)


def _rope_prep(proj, tab_c, tab_s1, tab_s2):
    bsz, seq, _ = proj.shape
    nblk = seq // MOBA_BLOCK
    hd = ATT_HEAD_DIM
    sec = lambda off: pl.BlockSpec((1, MOBA_BLOCK, hd), lambda b, h, i: (b, i, off // hd + h))
    tab = pl.BlockSpec((MOBA_BLOCK, hd), lambda b, h, i: (i, 0))
    return pl.pallas_call(
        _rope_kernel,
        out_shape=(
            jax.ShapeDtypeStruct((bsz, ATT_HEADS, nblk, hd, MOBA_BLOCK), F32),
            jax.ShapeDtypeStruct((bsz, ATT_HEADS, nblk, MOBA_BLOCK, hd), BF16),
            jax.ShapeDtypeStruct((bsz, ATT_HEADS, nblk, hd, MOBA_BLOCK), BF16),
            jax.ShapeDtypeStruct((bsz, ATT_HEADS, nblk, 1, hd), F32),
        ),
        grid=(bsz, ATT_HEADS, nblk),
        in_specs=[sec(OFF_Q), sec(OFF_K), sec(OFF_VA), tab, tab, tab],
        out_specs=(
            pl.BlockSpec((1, 1, 1, hd, MOBA_BLOCK), lambda b, h, i: (b, h, i, 0, 0)),
            pl.BlockSpec((1, 1, 1, MOBA_BLOCK, hd), lambda b, h, i: (b, h, i, 0, 0)),
            pl.BlockSpec((1, 1, 1, hd, MOBA_BLOCK), lambda b, h, i: (b, h, i, 0, 0)),
            pl.BlockSpec((1, 1, 1, 1, hd), lambda b, h, i: (b, h, i, 0, 0)),
        ),
        compiler_params=_params(("parallel", "parallel", "parallel")),
        name="rope_prep",
    )(proj, proj, proj, tab_c, tab_s1, tab_s2)


LOG2E = 1.4426950408889634


def _moba_kernel(qt_ref, kr_ref, vt_ref, km_ref, o_ref, sel_ref, qs_ref, acc_ref, *, nblk):
    i = pl.program_id(1)
    tq, nh, hd = MOBA_BLOCK, ATT_HEADS, ATT_HEAD_DIM
    blk = lax.broadcasted_iota(jnp.int32, (nblk, tq), 0)
    valid = blk < i
    krow = lax.broadcasted_iota(jnp.int32, (tq, tq), 0)
    qcol = lax.broadcasted_iota(jnp.int32, (tq, tq), 1)

    stats = []
    for h in range(nh):
        qt = qt_ref[0, h, 0]
        qs_ref[h] = (qt * (hd ** -0.5 * LOG2E)).astype(BF16)

        gate = jnp.dot(km_ref[0, h], qt, preferred_element_type=F32,
                       precision=lax.Precision.HIGHEST)
        gate = jnp.where(valid, gate, NEG_INF)
        cnt = jnp.zeros((nblk, tq), F32)
        for jp in range(nblk):
            gj = gate[jp:jp + 1, :]
            beats = (gj > gate) | ((gj == gate) & (blk > jp))
            cnt = cnt + jnp.where(beats, 1.0, 0.0)
        sel_ref[h] = jnp.where(valid & (cnt < MOBA_TOPK), 1.0, 0.0)

        s = jnp.dot(kr_ref[0, h, i], qs_ref[h], preferred_element_type=F32)
        s = jnp.where(krow <= qcol, s, NEG_INF)
        m0 = jnp.max(s, axis=0, keepdims=True)
        p = jnp.exp2(s - m0)
        acc_ref[h] = jnp.dot(vt_ref[0, h, i], p.astype(BF16), preferred_element_type=F32)
        stats += [m0, jnp.sum(p, axis=0, keepdims=True)]

    def scores(j):
        return [jnp.dot(kr_ref[0, h, j], qs_ref[h], preferred_element_type=F32) for h in range(nh)]

    def body(j, carry):
        s_next = scores(jnp.minimum(j + 1, nblk - 1))
        out = []
        for h in range(nh):
            m, l = carry[2 * h], carry[2 * h + 1]
            s = jnp.where(sel_ref[h, pl.ds(j, 1), :] > 0.5, carry[2 * nh + h], NEG_INF)
            m_new = jnp.maximum(m, jnp.max(s, axis=0, keepdims=True))
            alpha = jnp.exp2(m - m_new)
            p = jnp.exp2(s - m_new)
            acc_ref[h] = alpha * acc_ref[h] + jnp.dot(vt_ref[0, h, j], p.astype(BF16),
                                                      preferred_element_type=F32)
            out += [m_new, alpha * l + jnp.sum(p, axis=0, keepdims=True)]
        return tuple(out + s_next)

    stats = lax.fori_loop(0, i, body, tuple(stats + scores(0)))
    for h in range(nh):
        o_ref[0, :, h * hd:(h + 1) * hd] = (acc_ref[h] / stats[2 * h + 1]).T.astype(BF16)


def _moba(qt, kr, vt, km):
    bsz, nh, nblk, hd, tq = qt.shape
    seq = nblk * tq
    full = lambda shape: pl.BlockSpec((1, nh) + shape, lambda b, i: (b, 0) + (0,) * len(shape))
    return pl.pallas_call(
        functools.partial(_moba_kernel, nblk=nblk),
        out_shape=jax.ShapeDtypeStruct((bsz, seq, nh * hd), BF16),
        grid=(bsz, nblk),
        in_specs=[
            pl.BlockSpec((1, nh, 1, hd, tq), lambda b, i: (b, 0, i, 0, 0)),
            full((nblk, tq, hd)), full((nblk, hd, tq)), full((nblk, hd)),
        ],
        out_specs=pl.BlockSpec((1, tq, nh * hd), lambda b, i: (b, i, 0)),
        scratch_shapes=[pltpu.VMEM((nh, nblk, tq), F32), pltpu.VMEM((nh, hd, tq), BF16),
                        pltpu.VMEM((nh, hd, tq), F32)],
        compiler_params=_params(("parallel", "arbitrary")),
        name="moba",
    )(qt, kr, vt, km)


def _softplus(x):
    return jnp.maximum(x, 0.0) + jnp.log1p(jnp.exp(-jnp.abs(x)))


def _ssd_kernel(xbc_ref, z_ref, dt_ref, cw_ref, cb_ref, dtb_ref, alog_ref, dsk_ref, nw_ref,
                o_ref, prev_ref, state_ref, y_ref):
    L, P, N = SSM_CHUNK, SSM_HEAD_DIM, SSM_STATE
    hi = lax.Precision.HIGHEST

    @pl.when(pl.program_id(1) == 0)
    def _():
        prev_ref[...] = jnp.zeros_like(prev_ref)
        state_ref[...] = jnp.zeros_like(state_ref)

    cur = xbc_ref[0]
    prev = prev_ref[...]
    crow = lax.broadcasted_iota(jnp.int32, cur.shape, 0)
    acc = cur * cw_ref[SSM_CONV - 1:SSM_CONV, :] + cb_ref[...]
    for j in range(1, SSM_CONV):
        shifted = pltpu.roll(jnp.where(crow >= L - j, prev, cur), j, 0)
        acc = acc + shifted * cw_ref[SSM_CONV - 1 - j:SSM_CONV - j, :]
    prev_ref[...] = cur
    xc = _silu(acc)
    xs = xc[:, :SSM_WIDTH]
    bm = xc[:, SSM_WIDTH:SSM_WIDTH + SSM_GROUPS * N].astype(BF16)
    cm = xc[:, SSM_WIDTH + SSM_GROUPS * N:].astype(BF16)
    xs_b = xs.astype(BF16)
    xs_t = xs.T

    dt = _softplus(dt_ref[0] + dtb_ref[...])
    da = dt * (-jnp.exp(alog_ref[...]))
    row = lax.broadcasted_iota(jnp.int32, (L, L), 0)
    col = lax.broadcasted_iota(jnp.int32, (L, L), 1)
    causal = row >= col
    acum = jnp.dot(jnp.where(causal, 1.0, 0.0), da, preferred_element_type=F32, precision=hi)
    acum_t = jnp.dot(da.T, jnp.where(row <= col, 1.0, 0.0), preferred_element_type=F32, precision=hi)
    dt_t = dt.T
    last_t = acum_t[:, L - 1:L]
    w_end_t = jnp.exp(last_t - acum_t) * dt_t
    e_last_t = jnp.exp(last_t)
    e_acum = jnp.exp(acum)

    for g in range(SSM_GROUPS):
        b_g = bm[:, g * N:(g + 1) * N]
        c_g = cm[:, g * N:(g + 1) * N]
        cb = lax.dot_general(c_g, b_g, (((1,), (1,)), ((), ())), preferred_element_type=F32)
        for hh in range(SSM_HEADS // SSM_GROUPS):
            h = g * (SSM_HEADS // SSM_GROUPS) + hh
            hs = slice(h * P, (h + 1) * P)
            decay = jnp.exp(jnp.where(causal, acum[:, h:h + 1] - acum_t[h:h + 1, :], -jnp.inf))
            mat = (cb * decay * dt_t[h:h + 1, :]).astype(BF16)
            y = jnp.dot(mat, xs_b[:, hs], preferred_element_type=F32)
            st = state_ref[hs, :]
            y = y + lax.dot_general(c_g, st.astype(BF16), (((1,), (1,)), ((), ())),
                                    preferred_element_type=F32) * e_acum[:, h:h + 1]
            y_ref[:, hs] = y
            xw = (xs_t[hs, :] * w_end_t[h:h + 1, :]).astype(BF16)
            state_ref[hs, :] = st * e_last_t[h:h + 1, :] + jnp.dot(xw, b_g, preferred_element_type=F32)

    y = y_ref[...] + dsk_ref[...] * xs
    y = y * _silu(z_ref[0])
    gw = SSM_WIDTH // SSM_GROUPS
    for g in range(SSM_GROUPS):
        cols = slice(g * gw, (g + 1) * gw)
        yg = y[:, cols]
        yg = yg * lax.rsqrt(jnp.mean(yg * yg, axis=-1, keepdims=True) + NORM_EPS)
        o_ref[0, :, cols] = (yg * nw_ref[:, cols]).astype(BF16)


def _ssd(proj, dt_raw, conv_w, conv_b, dt_bias, a_log, d_skip, ssm_norm):
    bsz, seq, _ = proj.shape
    L = SSM_CHUNK
    vec = lambda n: pl.BlockSpec((1, n), lambda b, c: (0, 0))
    return pl.pallas_call(
        _ssd_kernel,
        out_shape=jax.ShapeDtypeStruct((bsz, seq, SSM_WIDTH), BF16),
        grid=(bsz, seq // L),
        in_specs=[
            pl.BlockSpec((1, L, SSM_CONV_DIM), lambda b, c: (b, c, 0)),
            pl.BlockSpec((1, L, SSM_WIDTH), lambda b, c: (b, c, OFF_Z // SSM_WIDTH)),
            pl.BlockSpec((1, L, LANES), lambda b, c: (b, c, 0)),
            pl.BlockSpec((SSM_CONV, SSM_CONV_DIM), lambda b, c: (0, 0)),
            vec(SSM_CONV_DIM), vec(LANES), vec(LANES), vec(SSM_WIDTH), vec(SSM_WIDTH),
        ],
        out_specs=pl.BlockSpec((1, L, SSM_WIDTH), lambda b, c: (b, c, 0)),
        scratch_shapes=[
            pltpu.VMEM((L, SSM_CONV_DIM), F32),
            pltpu.VMEM((SSM_WIDTH, SSM_STATE), F32),
            pltpu.VMEM((L, SSM_WIDTH), F32),
        ],
        compiler_params=_params(("parallel", "arbitrary")),
        name="ssd",
    )(proj, proj, dt_raw, conv_w, conv_b, dt_bias, a_log, d_skip, ssm_norm)


def _route(logits):
    lane = lax.broadcasted_iota(jnp.int32, logits.shape, 1).astype(F32)
    big = float(LANES)

    def first_argmax(vals, mask):
        top = jnp.max(vals, axis=-1, keepdims=True)
        idx = jnp.min(jnp.where(mask & (vals >= top), lane, big), axis=-1, keepdims=True)
        return top, idx

    def masked_softmax(mask):
        z = jnp.where(mask, logits, -jnp.inf)
        e = jnp.exp(z - jnp.max(z, axis=-1, keepdims=True))
        return e / jnp.sum(e, axis=-1, keepdims=True)

    gmask = lane < MOE_GROUPS
    g_prob, g_idx = first_argmax(jnp.where(gmask, masked_softmax(gmask), -1.0), gmask)
    lo = MOE_GROUPS + g_idx * MOE_EPG
    fmask = (lane >= lo) & (lane < lo + MOE_EPG)
    p_exp = jnp.where(fmask, masked_softmax(fmask), -1.0)
    p1, i1 = first_argmax(p_exp, fmask)
    rest = fmask & (lane != i1)
    p2, i2 = first_argmax(jnp.where(rest, p_exp, -1.0), rest)
    den = p1 + p2
    out = jnp.where(lane == 0, i1 - MOE_GROUPS, 0.0)
    out = jnp.where(lane == 1, i2 - MOE_GROUPS, out)
    out = jnp.where(lane == 2, g_prob * p1 / den, out)
    out = jnp.where(lane == 3, g_prob * p2 / den, out)
    return out


def _outproj_kernel(ya_ref, yb_ref, yc_ref, x_ref, g1_ref, wa_ref, wb_ref, wc_ref,
                    n2_ref, sc_ref, sh_ref, wr_ref, br_ref, xo_ref, hc_ref, rt_ref):
    mix = jnp.dot(ya_ref[0], wa_ref[...], preferred_element_type=F32)
    mix = mix + jnp.dot(yb_ref[0], wb_ref[...], preferred_element_type=F32)
    mix = mix + jnp.dot(yc_ref[0], wc_ref[...], preferred_element_type=F32)
    xn = x_ref[0] + g1_ref[0] * mix
    xo_ref[0] = xn
    h = _modulated_norm(xn, n2_ref[...], sc_ref[0], sh_ref[0])
    for c in range(N_CHUNK):
        hc_ref[0, c] = h[:, c * LANES:(c + 1) * LANES]
    h_hi = h.astype(BF16)
    h_lo = (h - h_hi.astype(F32)).astype(BF16)
    part = jnp.dot(h_hi, wr_ref[...], preferred_element_type=F32)
    logits = (part[:, :LANES] + part[:, LANES:]
              + jnp.dot(h_lo, wr_ref[:, :LANES], preferred_element_type=F32) + br_ref[...])
    rt_ref[0] = _route(logits)


TM_TOK = 512


def _out_proj_route(ya, yb, yc, x, g1, wa, wb, wc, n2, sc2, sh2, w_r, b_r):
    bsz, seq, d = x.shape
    tm = TM_TOK
    nt = seq // tm
    row = lambda w: pl.BlockSpec((1, tm, w), lambda b, i: (b, i, 0))
    per_b = pl.BlockSpec((1, 1, d), lambda b, i: (b, 0, 0))
    const = lambda a: pl.BlockSpec(a.shape, lambda b, i: (0,) * a.ndim,
                                   pipeline_mode=pl.Buffered(1))
    return pl.pallas_call(
        _outproj_kernel,
        out_shape=(jax.ShapeDtypeStruct((bsz, seq, d), F32),
                   jax.ShapeDtypeStruct((bsz * nt, N_CHUNK, tm, LANES), F32),
                   jax.ShapeDtypeStruct((bsz, seq, LANES), F32)),
        grid=(bsz, nt),
        in_specs=[row(ya.shape[-1]), row(yb.shape[-1]), row(yc.shape[-1]), row(d), per_b,
                  const(wa), const(wb), const(wc), const(n2), per_b, per_b, const(w_r), const(b_r)],
        out_specs=(row(d),
                   pl.BlockSpec((1, N_CHUNK, tm, LANES), lambda b, i: (b * nt + i, 0, 0, 0)),
                   row(LANES)),
        compiler_params=_params(("parallel", "parallel")),
        name="out_proj_route",
    )(ya, yb, yc, x, g1, wa, wb, wc, n2, sc2, sh2, w_r, b_r)


BLOCK_SHIFT = MOE_BLOCK.bit_length() - 1


TOK_SHIFT = TM_TOK.bit_length() - 1
GATHER_UNROLL = 8


def _expert_kernel(be_ref, tok_ref, h_hbm, wg_ref, wu_ref, wd_ref, o_ref,
                   buf_ref, wgb_ref, wub_ref, wdb_ref, sem):
    i = pl.program_id(0)
    slot = i % 2

    def gather(blk, slot_, mi):
        tok = tok_ref[blk * MOE_BLOCK + mi]
        return pltpu.make_async_copy(
            h_hbm.at[jnp.right_shift(tok, TOK_SHIFT), :, jnp.bitwise_and(tok, TM_TOK - 1), :],
            buf_ref.at[slot_, :, mi, :], sem.at[slot_])

    def for_rows(fn):
        def body(r, carry):
            for u in range(GATHER_UNROLL):
                fn(r * GATHER_UNROLL + u)
            return carry
        lax.fori_loop(0, MOE_BLOCK // GATHER_UNROLL, body, 0)

    @pl.when(i == 0)
    def _():
        for_rows(lambda mi: gather(0, 0, mi).start())

    @pl.when(i + 1 < pl.num_programs(0))
    def _():
        for_rows(lambda mi: gather(i + 1, 1 - slot, mi).start())

    @pl.when((i == 0) | (be_ref[i] != be_ref[jnp.maximum(i - 1, 0)]))
    def _():
        wgb_ref[...] = wg_ref[0, 0].astype(BF16)
        wub_ref[...] = wu_ref[0, 0].astype(BF16)
        wdb_ref[...] = wd_ref[0, 0].astype(BF16)

    for_rows(lambda mi: gather(i, slot, mi).wait())
    x = jnp.concatenate([buf_ref[slot, c] for c in range(N_CHUNK)], axis=-1).astype(BF16)
    gate = jnp.dot(x, wgb_ref[...], preferred_element_type=F32)
    up = jnp.dot(x, wub_ref[...], preferred_element_type=F32)
    hid = (_silu(gate) * up).astype(BF16)
    y = jnp.dot(hid, wdb_ref[...], preferred_element_type=F32)
    for c in range(N_CHUNK):
        o_ref[0, c] = y[:, c * LANES:(c + 1) * LANES]


def _experts(layer, blk_exp, row_tok, h_chunks, w_gate, w_up, w_down):
    n_blocks = blk_exp.shape[0]
    d, ff = w_gate.shape[2], w_gate.shape[3]
    return pl.pallas_call(
        _expert_kernel,
        out_shape=jax.ShapeDtypeStruct((n_blocks, N_CHUNK, MOE_BLOCK, LANES), F32),
        grid_spec=pltpu.PrefetchScalarGridSpec(
            num_scalar_prefetch=2,
            grid=(n_blocks,),
            in_specs=[pl.BlockSpec(memory_space=pl.ANY)] + [
                pl.BlockSpec(shape, lambda i, be, rt: (layer, be[i], 0, 0), pipeline_mode=pl.Buffered(1))
                for shape in ((1, 1, d, ff), (1, 1, d, ff), (1, 1, ff, d))],
            out_specs=pl.BlockSpec((1, N_CHUNK, MOE_BLOCK, LANES), lambda i, be, rt: (i, 0, 0, 0)),
            scratch_shapes=[pltpu.VMEM((2, N_CHUNK, MOE_BLOCK, LANES), F32),
                            pltpu.VMEM((d, ff), BF16), pltpu.VMEM((d, ff), BF16),
                            pltpu.VMEM((ff, d), BF16), pltpu.SemaphoreType.DMA((2,))],
        ),
        compiler_params=_params(("arbitrary",)),
        name="moe_experts",
    )(blk_exp, row_tok, h_chunks, w_gate, w_up, w_down)


def _combine_kernel(dest_ref, y_hbm, x_ref, g2_ref, rt_ref, o_ref, buf_ref, sem):
    i = pl.program_id(0)
    tm = TM_TOK

    def gather(k, mi):
        d = dest_ref[(i * tm + mi) * 2 + k]
        return pltpu.make_async_copy(
            y_hbm.at[jnp.right_shift(d, BLOCK_SHIFT), :, jnp.bitwise_and(d, MOE_BLOCK - 1), :],
            buf_ref.at[k, :, mi, :], sem)

    def issue(mi, carry):
        for k in range(2):
            gather(k, mi).start()
        return carry

    def drain(mi, carry):
        for k in range(2):
            gather(k, mi).wait()
        return carry

    lax.fori_loop(0, tm, issue, 0)
    lax.fori_loop(0, tm, drain, 0)
    rt = rt_ref[...]
    gate0, gate1 = rt[:, 2:3], rt[:, 3:4]
    for c in range(N_CHUNK):
        cols = slice(c * LANES, (c + 1) * LANES)
        moe = buf_ref[0, c] * gate0 + buf_ref[1, c] * gate1
        o_ref[:, cols] = x_ref[:, cols] + g2_ref[0, :, cols] * moe


def _combine(dest_flat, y_rows, x2d, g2, route2d, seq):
    n_tok, d = x2d.shape
    tm = TM_TOK
    per_seq = seq // tm
    return pl.pallas_call(
        _combine_kernel,
        out_shape=jax.ShapeDtypeStruct((n_tok, d), F32),
        grid_spec=pltpu.PrefetchScalarGridSpec(
            num_scalar_prefetch=1,
            grid=(n_tok // tm,),
            in_specs=[pl.BlockSpec(memory_space=pl.ANY),
                      pl.BlockSpec((tm, d), lambda i, dr: (i, 0)),
                      pl.BlockSpec((1, 1, d), lambda i, dr: (i // per_seq, 0, 0)),
                      pl.BlockSpec((tm, LANES), lambda i, dr: (i, 0))],
            out_specs=pl.BlockSpec((tm, d), lambda i, dr: (i, 0)),
            scratch_shapes=[pltpu.VMEM((2, N_CHUNK, tm, LANES), F32), pltpu.SemaphoreType.DMA],
        ),
        compiler_params=_params(("arbitrary",)),
        name="moe_combine",
    )(dest_flat, y_rows, x2d, g2, route2d)


def _dispatch_plan(route2d):
    n_tok = route2d.shape[0]
    n_asg = n_tok * 2
    flat_e = route2d[:, 0:2].astype(jnp.int32).reshape(n_asg)
    asg = jnp.arange(n_asg, dtype=jnp.int32)
    s_e, order = lax.sort((flat_e, asg), num_keys=1)
    ids = jnp.arange(MOE_EXPERTS, dtype=jnp.int32)
    start = jnp.sum(s_e[:, None] < ids[None, :], axis=0, dtype=jnp.int32)
    counts = jnp.sum(s_e[:, None] == ids[None, :], axis=0, dtype=jnp.int32)
    pcounts = (counts + MOE_BLOCK - 1) // MOE_BLOCK * MOE_BLOCK
    pend = jnp.cumsum(pcounts)
    pstart = pend - pcounts
    dest_sorted = pstart[s_e] + asg - start[s_e]
    _, dest = lax.sort((order, dest_sorted), num_keys=1)
    n_blocks = n_asg // MOE_BLOCK + MOE_EXPERTS
    blk_start = jnp.arange(n_blocks, dtype=jnp.int32) * MOE_BLOCK
    blk_exp = jnp.minimum(jnp.sum(pend[None, :] <= blk_start[:, None], axis=1, dtype=jnp.int32),
                          MOE_EXPERTS - 1)
    row = jnp.arange(n_blocks * MOE_BLOCK, dtype=jnp.int32)
    row_e = jnp.repeat(blk_exp, MOE_BLOCK)
    off = row - pstart[row_e]
    src = jnp.minimum(start[row_e] + off, n_asg - 1)
    row_tok = jnp.where(off < counts[row_e], order[src] // 2, 0).astype(jnp.int32)
    return dest, row_tok, blk_exp


def _final_norm_kernel(x_ref, g_ref, o_ref):
    x = x_ref[...]
    ms = jnp.mean(x * x, axis=-1, keepdims=True)
    o_ref[...] = (x * lax.rsqrt(ms + NORM_EPS)) * g_ref[...]


def _final_norm(x2d, g):
    n_tok, d = x2d.shape
    tm = 512
    return pl.pallas_call(
        _final_norm_kernel,
        out_shape=jax.ShapeDtypeStruct((n_tok, d), F32),
        grid=(n_tok // tm,),
        in_specs=[pl.BlockSpec((tm, d), lambda i: (i, 0)), pl.BlockSpec((1, d), lambda i: (0, 0))],
        out_specs=pl.BlockSpec((tm, d), lambda i: (i, 0)),
        compiler_params=_params(("parallel",)),
        name="final_norm",
    )(x2d, g)


def _rotary_tables(seq):
    pos = jnp.arange(seq, dtype=F32)
    inv_freq = ROPE_THETA ** (-jnp.arange(0, ROPE_DIM, 2, dtype=F32) / ROPE_DIM)
    ang = pos[:, None] * inv_freq[None, :]
    cos, sin = jnp.cos(ang), jnp.sin(ang)
    half = ROPE_DIM // 2
    ones = jnp.ones((seq, LANES - ROPE_DIM), F32)
    tab_c = jnp.concatenate([cos, cos, ones], axis=-1)
    tab_s1 = jnp.concatenate([-sin, jnp.zeros((seq, LANES - half), F32)], axis=-1)
    tab_s2 = jnp.concatenate([jnp.zeros((seq, half), F32), sin,
                              jnp.zeros((seq, LANES - ROPE_DIM), F32)], axis=-1)
    return tab_c, tab_s1, tab_s2


def _pad_lanes(v):
    return jnp.pad(v, (0, LANES - v.shape[0])).reshape(1, LANES)


def kernel(x, c, norm1, norm2, w_ada, b_ada, w_in, w_out, sgu_ln_g, sgu_ln_b, sgu_w, sgu_b, conv_w, conv_b, dt_bias, a_log, d_skip, ssm_norm, w_coarse, b_coarse, w_fine, b_fine, w_gate, w_up, w_down, final_norm):
    bsz, seq, d = x.shape
    depth = w_in.shape[0]
    n_tok = bsz * seq

    c_pad = jnp.pad(c, ((0, 8 - bsz % 8 if bsz % 8 else 0), (0, 0)))
    mod = _ada_mod(c_pad, w_ada, b_ada)[:, :bsz].reshape(depth, bsz, 6, 1, d)
    tab_c, tab_s1, tab_s2 = _rotary_tables(seq)

    for l in range(depth):
        sh1, sc1, g1, sh2, sc2, g2 = [mod[l, :, i] for i in range(6)]
        wl = w_in[l]
        w_main = jnp.concatenate(
            [wl[:, 3584:5120], wl[:, 0:512], wl[:, 2560:3584], wl[:, 512:1024], wl[:, 1024:2560]],
            axis=1).astype(BF16)
        w_dt = jnp.pad(wl[:, 5120:], ((0, 0), (0, LANES - SSM_HEADS))).astype(BF16)
        proj, dt_raw = _in_proj(x, norm1[l].reshape(1, d), sc1, sh1, w_main, w_dt)

        ya = _sgu(proj, sgu_ln_g[l].reshape(1, -1), sgu_ln_b[l].reshape(1, -1), sgu_w[l], sgu_b[l].T)
        qt, kr, vt, km = _rope_prep(proj, tab_c, tab_s1, tab_s2)
        yb = _moba(qt, kr, vt, km.reshape(km.shape[:3] + (ATT_HEAD_DIM,)))
        yc = _ssd(proj, dt_raw, conv_w[l], conv_b[l].reshape(1, -1), _pad_lanes(dt_bias[l]),
                  _pad_lanes(a_log[l]), jnp.repeat(d_skip[l], SSM_HEAD_DIM).reshape(1, -1),
                  ssm_norm[l].reshape(1, -1))

        wo = w_out[l].astype(BF16)
        w_r = jnp.concatenate([w_coarse[l], w_fine[l].transpose(1, 0, 2).reshape(d, MOE_EXPERTS)], axis=1)
        w_r = jnp.pad(w_r, ((0, 0), (0, LANES - w_r.shape[1])))
        w_r_hi = w_r.astype(BF16)
        w_r = jnp.concatenate([w_r_hi, (w_r - w_r_hi.astype(F32)).astype(BF16)], axis=1)
        b_r = _pad_lanes(jnp.concatenate([b_coarse[l], b_fine[l].reshape(-1)]))
        x, h_chunks, route = _out_proj_route(
            ya, yb, yc, x, g1, wo[:SGU_WIDTH], wo[SGU_WIDTH:2 * SGU_WIDTH], wo[2 * SGU_WIDTH:],
            norm2[l].reshape(1, d), sc2, sh2, w_r, b_r)

        route2d = route.reshape(n_tok, LANES)
        dest, row_tok, blk_exp = _dispatch_plan(route2d)
        y_rows = _experts(l, blk_exp, row_tok, h_chunks, w_gate, w_up, w_down)
        x = _combine(dest, y_rows, x.reshape(n_tok, d), g2, route2d, seq).reshape(bsz, seq, d)

    return _final_norm(x.reshape(n_tok, d), final_norm.reshape(1, d)).reshape(bsz, seq, d)
```

```python
import functools

import jax
import jax.numpy as jnp
from jax import lax
from jax.experimental import pallas as pl
from jax.experimental.pallas import tpu as pltpu

F32 = jnp.float32
BF16 = jnp.bfloat16

LANES = 128
NORM_EPS = 1e-6
NEG_INF = -1e30

D_MODEL = 2048
N_CHUNK = D_MODEL // LANES
N_PACK = N_CHUNK // 2
SGU_WIDTH = 512
SGU_GROUPS = 4
SGU_CHUNK = 128
ATT_HEADS = 4
ATT_HEAD_DIM = 128
ROPE_DIM = 32
ROPE_THETA = 500000.0
MOBA_BLOCK = 256
MOBA_TOPK = 3
SSM_WIDTH = 1024
SSM_HEAD_DIM = 64
SSM_HEADS = 16
SSM_GROUPS = 2
SSM_STATE = 128
SSM_CONV = 4
SSM_CHUNK = 128
SSM_CONV_DIM = 1536
MOE_GROUPS = 4
MOE_EPG = 8
MOE_EXPERTS = 32
MOE_FF = 512
MOE_BLOCK = 512

PROJ_MAIN = 5120
OFF_U, OFF_Z, OFF_V, OFF_Q, OFF_K, OFF_VA = 1536, 2048, 3072, 3584, 4096, 4608

VMEM_LIMIT = 56 * 1024 * 1024


def _params(sem, vmem=VMEM_LIMIT):
    return pltpu.CompilerParams(dimension_semantics=sem, vmem_limit_bytes=vmem)


def _silu(x):
    return x * jax.nn.sigmoid(x)


U32 = jnp.uint32
HI_MASK = 0xFFFF0000


def _pack_halves(lo, hi):
    lo_bits = lax.bitcast_convert_type(lo.astype(BF16).astype(F32), U32)
    hi_bits = lax.bitcast_convert_type(hi.astype(BF16).astype(F32), U32)
    return jnp.bitwise_or(jnp.bitwise_and(hi_bits, U32(HI_MASK)), jnp.right_shift(lo_bits, U32(16)))


def _unpack_halves(w):
    lo = lax.bitcast_convert_type(jnp.left_shift(w, U32(16)), F32)
    hi = lax.bitcast_convert_type(jnp.bitwise_and(w, U32(HI_MASK)), F32)
    return lo, hi


def _ada_kernel(c_ref, w_ref, b_ref, o_ref):
    ca = _silu(c_ref[...]).astype(BF16)
    o_ref[0] = jnp.dot(ca, w_ref[0].astype(BF16), preferred_element_type=F32) + b_ref[0]


def _ada_mod(c_pad, w_ada, b_ada):
    depth, d, n = w_ada.shape
    tn = 1024
    return pl.pallas_call(
        _ada_kernel,
        out_shape=jax.ShapeDtypeStruct((depth, c_pad.shape[0], n), F32),
        grid=(depth, n // tn),
        in_specs=[
            pl.BlockSpec(c_pad.shape, lambda l, j: (0, 0)),
            pl.BlockSpec((1, d, tn), lambda l, j: (l, 0, j)),
            pl.BlockSpec((1, 1, tn), lambda l, j: (l, 0, j)),
        ],
        out_specs=pl.BlockSpec((1, c_pad.shape[0], tn), lambda l, j: (l, 0, j)),
        compiler_params=_params(("parallel", "parallel")),
        name="ada_mod",
    )(c_pad, w_ada, b_ada.reshape(depth, 1, n))


def _modulated_norm(x, g, sc, sh):
    ms = jnp.mean(x * x, axis=-1, keepdims=True)
    return (x * lax.rsqrt(ms + NORM_EPS)) * g * (1.0 + sc) + sh


def _inproj_kernel(x_ref, g_ref, sc_ref, sh_ref, w_ref, wdt_ref, o_ref, dt_ref, h_ref):
    @pl.when(pl.program_id(2) == 0)
    def _():
        h = _modulated_norm(x_ref[0], g_ref[...], sc_ref[0], sh_ref[0]).astype(BF16)
        h_ref[...] = h
        dt_ref[0] = jnp.dot(h, wdt_ref[...], preferred_element_type=F32)

    o_ref[0] = jnp.dot(h_ref[...], w_ref[...], preferred_element_type=F32)


def _in_proj(x, g, sc, sh, w, wdt):
    bsz, seq, d = x.shape
    n = w.shape[1]
    tm, tn = 512, 1024
    return pl.pallas_call(
        _inproj_kernel,
        out_shape=(jax.ShapeDtypeStruct((bsz, seq, n), F32),
                   jax.ShapeDtypeStruct((bsz, seq, LANES), F32)),
        grid=(bsz, seq // tm, n // tn),
        in_specs=[
            pl.BlockSpec((1, tm, d), lambda b, i, j: (b, i, 0)),
            pl.BlockSpec((1, d), lambda b, i, j: (0, 0)),
            pl.BlockSpec((1, 1, d), lambda b, i, j: (b, 0, 0)),
            pl.BlockSpec((1, 1, d), lambda b, i, j: (b, 0, 0)),
            pl.BlockSpec((d, tn), lambda b, i, j: (0, j)),
            pl.BlockSpec((d, LANES), lambda b, i, j: (0, 0)),
        ],
        out_specs=(pl.BlockSpec((1, tm, tn), lambda b, i, j: (b, i, j)),
                   pl.BlockSpec((1, tm, LANES), lambda b, i, j: (b, i, 0))),
        scratch_shapes=[pltpu.VMEM((tm, d), BF16)],
        compiler_params=_params(("parallel", "parallel", "arbitrary")),
        name="in_proj",
    )(x, g, sc, sh, w, wdt)


def _sgu_kernel(u_ref, v_ref, g_ref, b_ref, w_ref, bs_ref, o_ref, *, n_chunks):
    row = lax.broadcasted_iota(jnp.int32, (SGU_CHUNK, SGU_CHUNK), 0)
    col = lax.broadcasted_iota(jnp.int32, (SGU_CHUNK, SGU_CHUNK), 1)
    w_causal = [jnp.where(row >= col, w_ref[g], 0.0).astype(BF16) for g in range(SGU_GROUPS)]
    for ci in range(n_chunks):
        rows = slice(ci * SGU_CHUNK, (ci + 1) * SGU_CHUNK)
        u = jax.nn.gelu(u_ref[0, rows, :])
        vf = jax.nn.gelu(v_ref[0, rows, :])
        mu = jnp.mean(vf, axis=-1, keepdims=True)
        var = jnp.mean(jnp.square(vf - mu), axis=-1, keepdims=True)
        vn = ((vf - mu) * lax.rsqrt(var + NORM_EPS) * g_ref[...] + b_ref[...]).astype(BF16)
        for g in range(SGU_GROUPS):
            cols = slice(g * LANES, (g + 1) * LANES)
            s = jnp.dot(w_causal[g], vn[:, cols], preferred_element_type=F32) + bs_ref[:, g:g + 1]
            o_ref[0, rows, cols] = (u[:, cols] * s).astype(BF16)


def _sgu(proj, ln_g, ln_b, w_s, b_s_t):
    bsz, seq, _ = proj.shape
    tc = 512
    blk = lambda off: pl.BlockSpec((1, tc, SGU_WIDTH), lambda b, i: (b, i, off // SGU_WIDTH))
    return pl.pallas_call(
        functools.partial(_sgu_kernel, n_chunks=tc // SGU_CHUNK),
        out_shape=jax.ShapeDtypeStruct((bsz, seq, SGU_WIDTH), BF16),
        grid=(bsz, seq // tc),
        in_specs=[
            blk(OFF_U), blk(OFF_V),
            pl.BlockSpec((1, SGU_WIDTH), lambda b, i: (0, 0)),
            pl.BlockSpec((1, SGU_WIDTH), lambda b, i: (0, 0)),
            pl.BlockSpec((SGU_GROUPS, SGU_CHUNK, SGU_CHUNK), lambda b, i: (0, 0, 0)),
            pl.BlockSpec((SGU_CHUNK, SGU_GROUPS), lambda b, i: (0, 0)),
        ],
        out_specs=pl.BlockSpec((1, tc, SGU_WIDTH), lambda b, i: (b, i, 0)),
        compiler_params=_params(("parallel", "parallel")),
        name="sgu",
    )(proj, proj, ln_g, ln_b, w_s, b_s_t)


def _rotary(t, c, s1, s2):
    half = ROPE_DIM // 2
    return t * c + pltpu.roll(t, LANES - half, 1) * s1 + pltpu.roll(t, half, 1) * s2


def _rope_kernel(q_ref, k_ref, v_ref, c_ref, s1_ref, s2_ref, qt_ref, kr_ref, vt_ref, km_ref):
    c, s1, s2 = c_ref[...], s1_ref[...], s2_ref[...]
    for h in range(ATT_HEADS):
        cols = slice(h * ATT_HEAD_DIM, (h + 1) * ATT_HEAD_DIM)
        q = _rotary(q_ref[0, :, cols], c, s1, s2)
        k = _rotary(k_ref[0, :, cols], c, s1, s2)
        qt_ref[0, h, 0] = q.T
        kr_ref[0, h, 0] = k.astype(BF16)
        vt_ref[0, h, 0] = v_ref[0, :, cols].T.astype(BF16)
        km_ref[0, h, 0] = jnp.mean(k, axis=0, keepdims=True)


def _rope_prep(proj, tab_c, tab_s1, tab_s2):
    bsz, seq, _ = proj.shape
    nblk = seq // MOBA_BLOCK
    hd, nh = ATT_HEAD_DIM, ATT_HEADS
    width = nh * hd
    sec = lambda off: pl.BlockSpec((1, MOBA_BLOCK, width), lambda i, b: (b, i, off // width))
    tab = pl.BlockSpec((MOBA_BLOCK, hd), lambda i, b: (i, 0))
    out = lambda rows, cols: pl.BlockSpec((1, nh, 1, rows, cols), lambda i, b: (b, 0, i, 0, 0))
    return pl.pallas_call(
        _rope_kernel,
        out_shape=(
            jax.ShapeDtypeStruct((bsz, nh, nblk, hd, MOBA_BLOCK), F32),
            jax.ShapeDtypeStruct((bsz, nh, nblk, MOBA_BLOCK, hd), BF16),
            jax.ShapeDtypeStruct((bsz, nh, nblk, hd, MOBA_BLOCK), BF16),
            jax.ShapeDtypeStruct((bsz, nh, nblk, 1, hd), F32),
        ),
        grid=(nblk, bsz),
        in_specs=[sec(OFF_Q), sec(OFF_K), sec(OFF_VA), tab, tab, tab],
        out_specs=(out(hd, MOBA_BLOCK), out(MOBA_BLOCK, hd), out(hd, MOBA_BLOCK), out(1, hd)),
        compiler_params=_params(("parallel", "parallel")),
        name="rope_prep",
    )(proj, proj, proj, tab_c, tab_s1, tab_s2)


LOG2E = 1.4426950408889634


def _moba_kernel(qt_ref, kr_ref, vt_ref, km_ref, o_ref, sel_ref, qs_ref, acc_ref, *, nblk):
    i = pl.program_id(1)
    tq, nh, hd = MOBA_BLOCK, ATT_HEADS, ATT_HEAD_DIM
    blk = lax.broadcasted_iota(jnp.int32, (nblk, tq), 0)
    valid = blk < i
    krow = lax.broadcasted_iota(jnp.int32, (tq, tq), 0)
    qcol = lax.broadcasted_iota(jnp.int32, (tq, tq), 1)

    stats = []
    for h in range(nh):
        qt = qt_ref[0, h, 0]
        qs_ref[h] = (qt * (hd ** -0.5 * LOG2E)).astype(BF16)

        gate = jnp.dot(km_ref[0, h], qt, preferred_element_type=F32,
                       precision=lax.Precision.HIGHEST)
        gate = jnp.where(valid, gate, NEG_INF)
        cnt = jnp.zeros((nblk, tq), F32)
        for jp in range(nblk):
            gj = gate[jp:jp + 1, :]
            beats = (gj > gate) | ((gj == gate) & (blk > jp))
            cnt = cnt + jnp.where(beats, 1.0, 0.0)
        sel_ref[h] = jnp.where(valid & (cnt < MOBA_TOPK), 1.0, 0.0)

        s = jnp.dot(kr_ref[0, h, i], qs_ref[h], preferred_element_type=F32)
        s = jnp.where(krow <= qcol, s, NEG_INF)
        m0 = jnp.max(s, axis=0, keepdims=True)
        p = jnp.exp2(s - m0)
        acc_ref[h] = jnp.dot(vt_ref[0, h, i], p.astype(BF16), preferred_element_type=F32)
        stats += [m0, jnp.sum(p, axis=0, keepdims=True)]

    def scores(j):
        return [jnp.dot(kr_ref[0, h, j], qs_ref[h], preferred_element_type=F32) for h in range(nh)]

    def body(j, carry):
        s_next = scores(jnp.minimum(j + 1, nblk - 1))
        out = []
        for h in range(nh):
            m, l = carry[2 * h], carry[2 * h + 1]
            s = jnp.where(sel_ref[h, pl.ds(j, 1), :] > 0.5, carry[2 * nh + h], NEG_INF)
            m_new = jnp.maximum(m, jnp.max(s, axis=0, keepdims=True))
            alpha = jnp.exp2(m - m_new)
            p = jnp.exp2(s - m_new)
            acc_ref[h] = alpha * acc_ref[h] + jnp.dot(vt_ref[0, h, j], p.astype(BF16),
                                                      preferred_element_type=F32)
            out += [m_new, alpha * l + jnp.sum(p, axis=0, keepdims=True)]
        return tuple(out + s_next)

    stats = lax.fori_loop(0, i, body, tuple(stats + scores(0)))
    for h in range(nh):
        o_ref[0, :, h * hd:(h + 1) * hd] = (acc_ref[h] / stats[2 * h + 1]).T.astype(BF16)


def _moba(qt, kr, vt, km):
    bsz, nh, nblk, hd, tq = qt.shape
    seq = nblk * tq
    full = lambda shape: pl.BlockSpec((1, nh) + shape, lambda b, i: (b, 0) + (0,) * len(shape))
    return pl.pallas_call(
        functools.partial(_moba_kernel, nblk=nblk),
        out_shape=jax.ShapeDtypeStruct((bsz, seq, nh * hd), BF16),
        grid=(bsz, nblk),
        in_specs=[
            pl.BlockSpec((1, nh, 1, hd, tq), lambda b, i: (b, 0, i, 0, 0)),
            full((nblk, tq, hd)), full((nblk, hd, tq)), full((nblk, hd)),
        ],
        out_specs=pl.BlockSpec((1, tq, nh * hd), lambda b, i: (b, i, 0)),
        scratch_shapes=[pltpu.VMEM((nh, nblk, tq), F32), pltpu.VMEM((nh, hd, tq), BF16),
                        pltpu.VMEM((nh, hd, tq), F32)],
        compiler_params=_params(("parallel", "arbitrary")),
        name="moba",
    )(qt, kr, vt, km)


def _softplus(x):
    return jnp.maximum(x, 0.0) + jnp.log1p(jnp.exp(-jnp.abs(x)))


def _ssd_kernel(xbc_ref, z_ref, dt_ref, cw_ref, cb_ref, dtb_ref, alog_ref, dsk_ref, nw_ref,
                o_ref, prev_ref, state_ref, y_ref):
    L, P, N = SSM_CHUNK, SSM_HEAD_DIM, SSM_STATE
    hi = lax.Precision.HIGHEST

    @pl.when(pl.program_id(1) == 0)
    def _():
        prev_ref[...] = jnp.zeros_like(prev_ref)
        state_ref[...] = jnp.zeros_like(state_ref)

    cur = xbc_ref[0]
    prev = prev_ref[...]
    crow = lax.broadcasted_iota(jnp.int32, cur.shape, 0)
    acc = cur * cw_ref[SSM_CONV - 1:SSM_CONV, :] + cb_ref[...]
    for j in range(1, SSM_CONV):
        shifted = pltpu.roll(jnp.where(crow >= L - j, prev, cur), j, 0)
        acc = acc + shifted * cw_ref[SSM_CONV - 1 - j:SSM_CONV - j, :]
    prev_ref[...] = cur
    xc = _silu(acc)
    xs = xc[:, :SSM_WIDTH]
    bm = xc[:, SSM_WIDTH:SSM_WIDTH + SSM_GROUPS * N].astype(BF16)
    cm = xc[:, SSM_WIDTH + SSM_GROUPS * N:].astype(BF16)
    xs_b = xs.astype(BF16)
    xs_t = xs.T

    dt = _softplus(dt_ref[0] + dtb_ref[...])
    da = dt * (-jnp.exp(alog_ref[...]))
    row = lax.broadcasted_iota(jnp.int32, (L, L), 0)
    col = lax.broadcasted_iota(jnp.int32, (L, L), 1)
    causal = row >= col
    acum = jnp.dot(jnp.where(causal, 1.0, 0.0), da, preferred_element_type=F32, precision=hi)
    acum_t = jnp.dot(da.T, jnp.where(row <= col, 1.0, 0.0), preferred_element_type=F32, precision=hi)
    dt_t = dt.T
    last_t = acum_t[:, L - 1:L]
    w_end_t = jnp.exp(last_t - acum_t) * dt_t
    e_last_t = jnp.exp(last_t)
    e_acum = jnp.exp(acum)

    for g in range(SSM_GROUPS):
        b_g = bm[:, g * N:(g + 1) * N]
        c_g = cm[:, g * N:(g + 1) * N]
        cb = lax.dot_general(c_g, b_g, (((1,), (1,)), ((), ())), preferred_element_type=F32)
        for hh in range(SSM_HEADS // SSM_GROUPS):
            h = g * (SSM_HEADS // SSM_GROUPS) + hh
            hs = slice(h * P, (h + 1) * P)
            decay = jnp.exp(jnp.where(causal, acum[:, h:h + 1] - acum_t[h:h + 1, :], -jnp.inf))
            mat = (cb * decay * dt_t[h:h + 1, :]).astype(BF16)
            y = jnp.dot(mat, xs_b[:, hs], preferred_element_type=F32)
            st = state_ref[hs, :]
            y = y + lax.dot_general(c_g, st.astype(BF16), (((1,), (1,)), ((), ())),
                                    preferred_element_type=F32) * e_acum[:, h:h + 1]
            y_ref[:, hs] = y
            xw = (xs_t[hs, :] * w_end_t[h:h + 1, :]).astype(BF16)
            state_ref[hs, :] = st * e_last_t[h:h + 1, :] + jnp.dot(xw, b_g, preferred_element_type=F32)

    y = y_ref[...] + dsk_ref[...] * xs
    y = y * _silu(z_ref[0])
    gw = SSM_WIDTH // SSM_GROUPS
    for g in range(SSM_GROUPS):
        cols = slice(g * gw, (g + 1) * gw)
        yg = y[:, cols]
        yg = yg * lax.rsqrt(jnp.mean(yg * yg, axis=-1, keepdims=True) + NORM_EPS)
        o_ref[0, :, cols] = (yg * nw_ref[:, cols]).astype(BF16)


def _ssd(proj, dt_raw, conv_w, conv_b, dt_bias, a_log, d_skip, ssm_norm):
    bsz, seq, _ = proj.shape
    L = SSM_CHUNK
    vec = lambda n: pl.BlockSpec((1, n), lambda b, c: (0, 0))
    return pl.pallas_call(
        _ssd_kernel,
        out_shape=jax.ShapeDtypeStruct((bsz, seq, SSM_WIDTH), BF16),
        grid=(bsz, seq // L),
        in_specs=[
            pl.BlockSpec((1, L, SSM_CONV_DIM), lambda b, c: (b, c, 0)),
            pl.BlockSpec((1, L, SSM_WIDTH), lambda b, c: (b, c, OFF_Z // SSM_WIDTH)),
            pl.BlockSpec((1, L, LANES), lambda b, c: (b, c, 0)),
            pl.BlockSpec((SSM_CONV, SSM_CONV_DIM), lambda b, c: (0, 0)),
            vec(SSM_CONV_DIM), vec(LANES), vec(LANES), vec(SSM_WIDTH), vec(SSM_WIDTH),
        ],
        out_specs=pl.BlockSpec((1, L, SSM_WIDTH), lambda b, c: (b, c, 0)),
        scratch_shapes=[
            pltpu.VMEM((L, SSM_CONV_DIM), F32),
            pltpu.VMEM((SSM_WIDTH, SSM_STATE), F32),
            pltpu.VMEM((L, SSM_WIDTH), F32),
        ],
        compiler_params=_params(("parallel", "arbitrary")),
        name="ssd",
    )(proj, proj, dt_raw, conv_w, conv_b, dt_bias, a_log, d_skip, ssm_norm)


def _route(logits):
    lane = lax.broadcasted_iota(jnp.int32, logits.shape, 1).astype(F32)
    big = float(LANES)

    def first_argmax(vals, mask):
        top = jnp.max(vals, axis=-1, keepdims=True)
        idx = jnp.min(jnp.where(mask & (vals >= top), lane, big), axis=-1, keepdims=True)
        return top, idx

    def masked_softmax(mask):
        z = jnp.where(mask, logits, -jnp.inf)
        e = jnp.exp(z - jnp.max(z, axis=-1, keepdims=True))
        return e / jnp.sum(e, axis=-1, keepdims=True)

    gmask = lane < MOE_GROUPS
    g_prob, g_idx = first_argmax(jnp.where(gmask, masked_softmax(gmask), -1.0), gmask)
    lo = MOE_GROUPS + g_idx * MOE_EPG
    fmask = (lane >= lo) & (lane < lo + MOE_EPG)
    p_exp = jnp.where(fmask, masked_softmax(fmask), -1.0)
    p1, i1 = first_argmax(p_exp, fmask)
    rest = fmask & (lane != i1)
    p2, i2 = first_argmax(jnp.where(rest, p_exp, -1.0), rest)
    den = p1 + p2
    out = jnp.where(lane == 0, i1 - MOE_GROUPS, 0.0)
    out = jnp.where(lane == 1, i2 - MOE_GROUPS, out)
    out = jnp.where(lane == 2, g_prob * p1 / den, out)
    out = jnp.where(lane == 3, g_prob * p2 / den, out)
    return out


def _outproj_kernel(ya_ref, yb_ref, yc_ref, x_ref, g1_ref, wa_ref, wb_ref, wc_ref,
                    n2_ref, sc_ref, sh_ref, wr_ref, br_ref, xo_ref, hc_ref, rt_ref):
    mix = jnp.dot(ya_ref[0], wa_ref[...], preferred_element_type=F32)
    mix = mix + jnp.dot(yb_ref[0], wb_ref[...], preferred_element_type=F32)
    mix = mix + jnp.dot(yc_ref[0], wc_ref[...], preferred_element_type=F32)
    xn = x_ref[0] + g1_ref[0] * mix
    xo_ref[0] = xn
    h = _modulated_norm(xn, n2_ref[...], sc_ref[0], sh_ref[0])
    half = N_PACK * LANES
    packed = _pack_halves(h[:, :half], h[:, half:])
    for c in range(N_PACK):
        hc_ref[0, c] = packed[:, c * LANES:(c + 1) * LANES]
    h_hi = h.astype(BF16)
    h_lo = (h - h_hi.astype(F32)).astype(BF16)
    part = jnp.dot(h_hi, wr_ref[...], preferred_element_type=F32)
    logits = (part[:, :LANES] + part[:, LANES:]
              + jnp.dot(h_lo, wr_ref[:, :LANES], preferred_element_type=F32) + br_ref[...])
    rt_ref[0] = _route(logits)


TM_TOK = 512


def _out_proj_route(ya, yb, yc, x, g1, wa, wb, wc, n2, sc2, sh2, w_r, b_r):
    bsz, seq, d = x.shape
    tm = TM_TOK
    nt = seq // tm
    row = lambda w: pl.BlockSpec((1, tm, w), lambda b, i: (b, i, 0))
    per_b = pl.BlockSpec((1, 1, d), lambda b, i: (b, 0, 0))
    const = lambda a: pl.BlockSpec(a.shape, lambda b, i: (0,) * a.ndim,
                                   pipeline_mode=pl.Buffered(1))
    return pl.pallas_call(
        _outproj_kernel,
        out_shape=(jax.ShapeDtypeStruct((bsz, seq, d), F32),
                   jax.ShapeDtypeStruct((bsz * nt, N_PACK, tm, LANES), U32),
                   jax.ShapeDtypeStruct((bsz, seq, LANES), F32)),
        grid=(bsz, nt),
        in_specs=[row(ya.shape[-1]), row(yb.shape[-1]), row(yc.shape[-1]), row(d), per_b,
                  const(wa), const(wb), const(wc), const(n2), per_b, per_b, const(w_r), const(b_r)],
        out_specs=(row(d),
                   pl.BlockSpec((1, N_PACK, tm, LANES), lambda b, i: (b * nt + i, 0, 0, 0)),
                   row(LANES)),
        compiler_params=_params(("parallel", "parallel")),
        name="out_proj_route",
    )(ya, yb, yc, x, g1, wa, wb, wc, n2, sc2, sh2, w_r, b_r)


BLOCK_SHIFT = MOE_BLOCK.bit_length() - 1


TOK_SHIFT = TM_TOK.bit_length() - 1
GATHER_UNROLL = 8
MXU_WIDTH = 256


def _expert_kernel(be_ref, tok_ref, h_hbm, wg_ref, wu_ref, wd_ref, o_ref,
                   buf_ref, x_ref, wgb_ref, wub_ref, wdb_ref, sem):
    i = pl.program_id(0)
    last = pl.num_programs(0) - 1
    slot = i % 2
    nxt = jnp.minimum(i + 1, last)
    half = N_PACK * LANES

    def gather(blk, slot_, mi):
        tok = tok_ref[blk * MOE_BLOCK + mi]
        return pltpu.make_async_copy(
            h_hbm.at[jnp.right_shift(tok, TOK_SHIFT), :, jnp.bitwise_and(tok, TM_TOK - 1), :],
            buf_ref.at[slot_, :, mi, :], sem.at[slot_])

    def for_rows(fn):
        def body(r, carry):
            for u in range(GATHER_UNROLL):
                fn(r * GATHER_UNROLL + u)
            return carry
        lax.fori_loop(0, MOE_BLOCK // GATHER_UNROLL, body, 0)

    @pl.when(i == 0)
    def _():
        for_rows(lambda mi: gather(0, 0, mi).start())

    @pl.when((i == 0) | (be_ref[i] != be_ref[jnp.maximum(i - 1, 0)]))
    def _():
        wgb_ref[...] = wg_ref[0, 0].astype(BF16)
        wub_ref[...] = wu_ref[0, 0].astype(BF16)
        wdb_ref[...] = wd_ref[0, 0].astype(BF16)

    for_rows(lambda mi: gather(i, slot, mi).wait())
    for c in range(N_PACK):
        lo, hi = _unpack_halves(buf_ref[slot, c])
        x_ref[:, c * LANES:(c + 1) * LANES] = lo.astype(BF16)
        x_ref[:, half + c * LANES:half + (c + 1) * LANES] = hi.astype(BF16)

    ff = wgb_ref.shape[1]
    n_up, n_down = ff // MXU_WIDTH, half // MXU_WIDTH
    rows_per_piece = MOE_BLOCK // (2 * n_up + n_down)
    pieces = iter(range(2 * n_up + n_down))

    def issue_piece():
        p = next(pieces)
        for mi in range(p * rows_per_piece, (p + 1) * rows_per_piece):
            gather(nxt, 1 - slot, mi).start()

    hid = []
    for p in range(n_up):
        cols = slice(p * MXU_WIDTH, (p + 1) * MXU_WIDTH)
        issue_piece()
        gate = jnp.dot(x_ref[...], wgb_ref[:, cols], preferred_element_type=F32)
        issue_piece()
        up = jnp.dot(x_ref[...], wub_ref[:, cols], preferred_element_type=F32)
        hid.append((_silu(gate) * up).astype(BF16))
    hid = jnp.concatenate(hid, axis=-1)
    for p in range(n_down):
        cols = slice(p * MXU_WIDTH, (p + 1) * MXU_WIDTH)
        issue_piece()
        y_lo = jnp.dot(hid, wdb_ref[:, cols], preferred_element_type=F32)
        y_hi = jnp.dot(hid, wdb_ref[:, half + p * MXU_WIDTH:half + (p + 1) * MXU_WIDTH],
                       preferred_element_type=F32)
        packed = _pack_halves(y_lo, y_hi)
        for c in range(MXU_WIDTH // LANES):
            o_ref[0, p * (MXU_WIDTH // LANES) + c] = packed[:, c * LANES:(c + 1) * LANES]

    @pl.when(i == last)
    def _():
        for_rows(lambda mi: gather(nxt, 1 - slot, mi).wait())


def _experts(layer, blk_exp, row_tok, h_chunks, w_gate, w_up, w_down):
    n_blocks = blk_exp.shape[0]
    d, ff = w_gate.shape[2], w_gate.shape[3]
    return pl.pallas_call(
        _expert_kernel,
        out_shape=jax.ShapeDtypeStruct((n_blocks, N_PACK, MOE_BLOCK, LANES), U32),
        grid_spec=pltpu.PrefetchScalarGridSpec(
            num_scalar_prefetch=2,
            grid=(n_blocks,),
            in_specs=[pl.BlockSpec(memory_space=pl.ANY)] + [
                pl.BlockSpec(shape, lambda i, be, rt: (layer, be[i], 0, 0))
                for shape in ((1, 1, d, ff), (1, 1, d, ff), (1, 1, ff, d))],
            out_specs=pl.BlockSpec((1, N_PACK, MOE_BLOCK, LANES), lambda i, be, rt: (i, 0, 0, 0)),
            scratch_shapes=[pltpu.VMEM((2, N_PACK, MOE_BLOCK, LANES), U32),
                            pltpu.VMEM((MOE_BLOCK, d), BF16),
                            pltpu.VMEM((d, ff), BF16), pltpu.VMEM((d, ff), BF16),
                            pltpu.VMEM((ff, d), BF16), pltpu.SemaphoreType.DMA((2,))],
        ),
        compiler_params=_params(("arbitrary",)),
        name="moe_experts",
    )(blk_exp, row_tok, h_chunks, w_gate, w_up, w_down)


def _combine_kernel(dest_ref, y_hbm, x_ref, g2_ref, rt_ref, o_ref, buf_ref, sem):
    i = pl.program_id(0)
    tm = TM_TOK

    def gather(k, mi):
        d = dest_ref[(i * tm + mi) * 2 + k]
        return pltpu.make_async_copy(
            y_hbm.at[jnp.right_shift(d, BLOCK_SHIFT), :, jnp.bitwise_and(d, MOE_BLOCK - 1), :],
            buf_ref.at[k, :, mi, :], sem)

    def issue(mi, carry):
        for k in range(2):
            gather(k, mi).start()
        return carry

    def drain(mi, carry):
        for k in range(2):
            gather(k, mi).wait()
        return carry

    lax.fori_loop(0, tm, issue, 0)
    lax.fori_loop(0, tm, drain, 0)
    rt = rt_ref[...]
    gate0, gate1 = rt[:, 2:3], rt[:, 3:4]
    for c in range(N_PACK):
        halves = zip(_unpack_halves(buf_ref[0, c]), _unpack_halves(buf_ref[1, c]))
        for part, (y0, y1) in enumerate(halves):
            cols = slice((part * N_PACK + c) * LANES, (part * N_PACK + c + 1) * LANES)
            o_ref[:, cols] = x_ref[:, cols] + g2_ref[0, :, cols] * (y0 * gate0 + y1 * gate1)


def _combine(dest_flat, y_rows, x2d, g2, route2d, seq):
    n_tok, d = x2d.shape
    tm = TM_TOK
    per_seq = seq // tm
    return pl.pallas_call(
        _combine_kernel,
        out_shape=jax.ShapeDtypeStruct((n_tok, d), F32),
        grid_spec=pltpu.PrefetchScalarGridSpec(
            num_scalar_prefetch=1,
            grid=(n_tok // tm,),
            in_specs=[pl.BlockSpec(memory_space=pl.ANY),
                      pl.BlockSpec((tm, d), lambda i, dr: (i, 0)),
                      pl.BlockSpec((1, 1, d), lambda i, dr: (i // per_seq, 0, 0)),
                      pl.BlockSpec((tm, LANES), lambda i, dr: (i, 0))],
            out_specs=pl.BlockSpec((tm, d), lambda i, dr: (i, 0)),
            scratch_shapes=[pltpu.VMEM((2, N_PACK, tm, LANES), U32), pltpu.SemaphoreType.DMA],
        ),
        compiler_params=_params(("arbitrary",)),
        name="moe_combine",
    )(dest_flat, y_rows, x2d, g2, route2d)


def _dispatch_plan(route2d):
    n_tok = route2d.shape[0]
    n_asg = n_tok * 2
    flat_e = route2d[:, 0:2].astype(jnp.int32).reshape(n_asg)
    asg = jnp.arange(n_asg, dtype=jnp.int32)
    s_e, order = lax.sort((flat_e, asg), num_keys=1)
    ids = jnp.arange(MOE_EXPERTS, dtype=jnp.int32)
    start = jnp.sum(s_e[:, None] < ids[None, :], axis=0, dtype=jnp.int32)
    counts = jnp.sum(s_e[:, None] == ids[None, :], axis=0, dtype=jnp.int32)
    pcounts = (counts + MOE_BLOCK - 1) // MOE_BLOCK * MOE_BLOCK
    pend = jnp.cumsum(pcounts)
    pstart = pend - pcounts
    dest_sorted = pstart[s_e] + asg - start[s_e]
    _, dest = lax.sort((order, dest_sorted), num_keys=1)
    n_blocks = n_asg // MOE_BLOCK + MOE_EXPERTS
    blk_start = jnp.arange(n_blocks, dtype=jnp.int32) * MOE_BLOCK
    blk_exp = jnp.minimum(jnp.sum(pend[None, :] <= blk_start[:, None], axis=1, dtype=jnp.int32),
                          MOE_EXPERTS - 1)
    row = jnp.arange(n_blocks * MOE_BLOCK, dtype=jnp.int32)
    row_e = jnp.repeat(blk_exp, MOE_BLOCK)
    off = row - pstart[row_e]
    src = jnp.minimum(start[row_e] + off, n_asg - 1)
    row_tok = jnp.where(off < counts[row_e], order[src] // 2, 0).astype(jnp.int32)
    return dest, row_tok, blk_exp


def _final_norm_kernel(x_ref, g_ref, o_ref):
    x = x_ref[...]
    ms = jnp.mean(x * x, axis=-1, keepdims=True)
    o_ref[...] = (x * lax.rsqrt(ms + NORM_EPS)) * g_ref[...]


def _final_norm(x2d, g):
    n_tok, d = x2d.shape
    tm = 512
    return pl.pallas_call(
        _final_norm_kernel,
        out_shape=jax.ShapeDtypeStruct((n_tok, d), F32),
        grid=(n_tok // tm,),
        in_specs=[pl.BlockSpec((tm, d), lambda i: (i, 0)), pl.BlockSpec((1, d), lambda i: (0, 0))],
        out_specs=pl.BlockSpec((tm, d), lambda i: (i, 0)),
        compiler_params=_params(("parallel",)),
        name="final_norm",
    )(x2d, g)


def _rotary_tables(seq):
    pos = jnp.arange(seq, dtype=F32)
    inv_freq = ROPE_THETA ** (-jnp.arange(0, ROPE_DIM, 2, dtype=F32) / ROPE_DIM)
    ang = pos[:, None] * inv_freq[None, :]
    cos, sin = jnp.cos(ang), jnp.sin(ang)
    half = ROPE_DIM // 2
    ones = jnp.ones((seq, LANES - ROPE_DIM), F32)
    tab_c = jnp.concatenate([cos, cos, ones], axis=-1)
    tab_s1 = jnp.concatenate([-sin, jnp.zeros((seq, LANES - half), F32)], axis=-1)
    tab_s2 = jnp.concatenate([jnp.zeros((seq, half), F32), sin,
                              jnp.zeros((seq, LANES - ROPE_DIM), F32)], axis=-1)
    return tab_c, tab_s1, tab_s2


def _pad_lanes(v):
    return jnp.pad(v, (0, LANES - v.shape[0])).reshape(1, LANES)


def kernel(x, c, norm1, norm2, w_ada, b_ada, w_in, w_out, sgu_ln_g, sgu_ln_b, sgu_w, sgu_b, conv_w, conv_b, dt_bias, a_log, d_skip, ssm_norm, w_coarse, b_coarse, w_fine, b_fine, w_gate, w_up, w_down, final_norm):
    bsz, seq, d = x.shape
    depth = w_in.shape[0]
    n_tok = bsz * seq

    c_pad = jnp.pad(c, ((0, 8 - bsz % 8 if bsz % 8 else 0), (0, 0)))
    mod = _ada_mod(c_pad, w_ada, b_ada)[:, :bsz].reshape(depth, bsz, 6, 1, d)
    tab_c, tab_s1, tab_s2 = _rotary_tables(seq)

    for l in range(depth):
        sh1, sc1, g1, sh2, sc2, g2 = [mod[l, :, i] for i in range(6)]
        wl = w_in[l]
        w_main = jnp.concatenate(
            [wl[:, 3584:5120], wl[:, 0:512], wl[:, 2560:3584], wl[:, 512:1024], wl[:, 1024:2560]],
            axis=1).astype(BF16)
        w_dt = jnp.pad(wl[:, 5120:], ((0, 0), (0, LANES - SSM_HEADS))).astype(BF16)
        proj, dt_raw = _in_proj(x, norm1[l].reshape(1, d), sc1, sh1, w_main, w_dt)

        ya = _sgu(proj, sgu_ln_g[l].reshape(1, -1), sgu_ln_b[l].reshape(1, -1), sgu_w[l], sgu_b[l].T)
        qt, kr, vt, km = _rope_prep(proj, tab_c, tab_s1, tab_s2)
        yb = _moba(qt, kr, vt, km.reshape(km.shape[:3] + (ATT_HEAD_DIM,)))
        yc = _ssd(proj, dt_raw, conv_w[l], conv_b[l].reshape(1, -1), _pad_lanes(dt_bias[l]),
                  _pad_lanes(a_log[l]), jnp.repeat(d_skip[l], SSM_HEAD_DIM).reshape(1, -1),
                  ssm_norm[l].reshape(1, -1))

        wo = w_out[l].astype(BF16)
        w_r = jnp.concatenate([w_coarse[l], w_fine[l].transpose(1, 0, 2).reshape(d, MOE_EXPERTS)], axis=1)
        w_r = jnp.pad(w_r, ((0, 0), (0, LANES - w_r.shape[1])))
        w_r_hi = w_r.astype(BF16)
        w_r = jnp.concatenate([w_r_hi, (w_r - w_r_hi.astype(F32)).astype(BF16)], axis=1)
        b_r = _pad_lanes(jnp.concatenate([b_coarse[l], b_fine[l].reshape(-1)]))
        x, h_chunks, route = _out_proj_route(
            ya, yb, yc, x, g1, wo[:SGU_WIDTH], wo[SGU_WIDTH:2 * SGU_WIDTH], wo[2 * SGU_WIDTH:],
            norm2[l].reshape(1, d), sc2, sh2, w_r, b_r)

        route2d = route.reshape(n_tok, LANES)
        dest, row_tok, blk_exp = _dispatch_plan(route2d)
        y_rows = _experts(l, blk_exp, row_tok, h_chunks, w_gate, w_up, w_down)
        x = _combine(dest, y_rows, x.reshape(n_tok, d), g2, route2d, seq).reshape(bsz, seq, d)

    return _final_norm(x.reshape(n_tok, d), final_norm.reshape(1, d)).reshape(bsz, seq, d)
```

```python
import functools

import jax
import jax.numpy as jnp
from jax import lax
from jax.experimental import pallas as pl
from jax.experimental.pallas import tpu as pltpu

F32 = jnp.float32
BF16 = jnp.bfloat16

LANES = 128
NORM_EPS = 1e-6
NEG_INF = -1e30

D_MODEL = 2048
N_CHUNK = D_MODEL // LANES
N_PACK = N_CHUNK // 2
SGU_WIDTH = 512
SGU_GROUPS = 4
SGU_CHUNK = 128
ATT_HEADS = 4
ATT_HEAD_DIM = 128
ROPE_DIM = 32
ROPE_THETA = 500000.0
MOBA_BLOCK = 256
MOBA_TOPK = 3
SSM_WIDTH = 1024
SSM_HEAD_DIM = 64
SSM_HEADS = 16
SSM_GROUPS = 2
SSM_STATE = 128
SSM_CONV = 4
SSM_CHUNK = 128
SSM_CONV_DIM = 1536
MOE_GROUPS = 4
MOE_EPG = 8
MOE_EXPERTS = 32
MOE_FF = 512
MOE_BLOCK = 512

PROJ_MAIN = 5120
OFF_U, OFF_Z, OFF_V, OFF_Q, OFF_K, OFF_VA = 1536, 2048, 3072, 3584, 4096, 4608

VMEM_LIMIT = 56 * 1024 * 1024


def _params(sem, vmem=VMEM_LIMIT):
    return pltpu.CompilerParams(dimension_semantics=sem, vmem_limit_bytes=vmem)


def _silu(x):
    return x * jax.nn.sigmoid(x)


U32 = jnp.uint32
HI_MASK = 0xFFFF0000


def _pack_halves(lo, hi):
    lo_bits = lax.bitcast_convert_type(lo.astype(BF16).astype(F32), U32)
    hi_bits = lax.bitcast_convert_type(hi.astype(BF16).astype(F32), U32)
    return jnp.bitwise_or(jnp.bitwise_and(hi_bits, U32(HI_MASK)), jnp.right_shift(lo_bits, U32(16)))


def _unpack_halves(w):
    lo = lax.bitcast_convert_type(jnp.left_shift(w, U32(16)), F32)
    hi = lax.bitcast_convert_type(jnp.bitwise_and(w, U32(HI_MASK)), F32)
    return lo, hi


def _ada_kernel(c_ref, w_ref, b_ref, o_ref):
    ca = _silu(c_ref[...]).astype(BF16)
    o_ref[0] = jnp.dot(ca, w_ref[0].astype(BF16), preferred_element_type=F32) + b_ref[0]


def _ada_mod(c_pad, w_ada, b_ada):
    depth, d, n = w_ada.shape
    tn = 1024
    return pl.pallas_call(
        _ada_kernel,
        out_shape=jax.ShapeDtypeStruct((depth, c_pad.shape[0], n), F32),
        grid=(depth, n // tn),
        in_specs=[
            pl.BlockSpec(c_pad.shape, lambda l, j: (0, 0)),
            pl.BlockSpec((1, d, tn), lambda l, j: (l, 0, j)),
            pl.BlockSpec((1, 1, tn), lambda l, j: (l, 0, j)),
        ],
        out_specs=pl.BlockSpec((1, c_pad.shape[0], tn), lambda l, j: (l, 0, j)),
        compiler_params=_params(("parallel", "parallel")),
        name="ada_mod",
    )(c_pad, w_ada, b_ada.reshape(depth, 1, n))


def _modulated_norm(x, g, sc, sh):
    ms = jnp.mean(x * x, axis=-1, keepdims=True)
    return (x * lax.rsqrt(ms + NORM_EPS)) * g * (1.0 + sc) + sh


def _inproj_kernel(x_ref, g_ref, sc_ref, sh_ref, w_ref, wdt_ref, o_ref, dt_ref, h_ref):
    @pl.when(pl.program_id(2) == 0)
    def _():
        h = _modulated_norm(x_ref[0], g_ref[...], sc_ref[0], sh_ref[0]).astype(BF16)
        h_ref[...] = h
        dt_ref[0] = jnp.dot(h, wdt_ref[...], preferred_element_type=F32)

    o_ref[0] = jnp.dot(h_ref[...], w_ref[...], preferred_element_type=F32)


def _in_proj(x, g, sc, sh, w, wdt):
    bsz, seq, d = x.shape
    n = w.shape[1]
    tm, tn = 512, 1024
    return pl.pallas_call(
        _inproj_kernel,
        out_shape=(jax.ShapeDtypeStruct((bsz, seq, n), F32),
                   jax.ShapeDtypeStruct((bsz, seq, LANES), F32)),
        grid=(bsz, seq // tm, n // tn),
        in_specs=[
            pl.BlockSpec((1, tm, d), lambda b, i, j: (b, i, 0)),
            pl.BlockSpec((1, d), lambda b, i, j: (0, 0)),
            pl.BlockSpec((1, 1, d), lambda b, i, j: (b, 0, 0)),
            pl.BlockSpec((1, 1, d), lambda b, i, j: (b, 0, 0)),
            pl.BlockSpec((d, tn), lambda b, i, j: (0, j)),
            pl.BlockSpec((d, LANES), lambda b, i, j: (0, 0)),
        ],
        out_specs=(pl.BlockSpec((1, tm, tn), lambda b, i, j: (b, i, j)),
                   pl.BlockSpec((1, tm, LANES), lambda b, i, j: (b, i, 0))),
        scratch_shapes=[pltpu.VMEM((tm, d), BF16)],
        compiler_params=_params(("parallel", "parallel", "arbitrary")),
        name="in_proj",
    )(x, g, sc, sh, w, wdt)


def _sgu_kernel(u_ref, v_ref, g_ref, b_ref, w_ref, bs_ref, o_ref, *, n_chunks):
    row = lax.broadcasted_iota(jnp.int32, (SGU_CHUNK, SGU_CHUNK), 0)
    col = lax.broadcasted_iota(jnp.int32, (SGU_CHUNK, SGU_CHUNK), 1)
    w_causal = [jnp.where(row >= col, w_ref[g], 0.0).astype(BF16) for g in range(SGU_GROUPS)]
    for ci in range(n_chunks):
        rows = slice(ci * SGU_CHUNK, (ci + 1) * SGU_CHUNK)
        u = jax.nn.gelu(u_ref[0, rows, :])
        vf = jax.nn.gelu(v_ref[0, rows, :])
        mu = jnp.mean(vf, axis=-1, keepdims=True)
        var = jnp.mean(jnp.square(vf - mu), axis=-1, keepdims=True)
        vn = ((vf - mu) * lax.rsqrt(var + NORM_EPS) * g_ref[...] + b_ref[...]).astype(BF16)
        for g in range(SGU_GROUPS):
            cols = slice(g * LANES, (g + 1) * LANES)
            s = jnp.dot(w_causal[g], vn[:, cols], preferred_element_type=F32) + bs_ref[:, g:g + 1]
            o_ref[0, rows, cols] = (u[:, cols] * s).astype(BF16)


def _sgu(proj, ln_g, ln_b, w_s, b_s_t):
    bsz, seq, _ = proj.shape
    tc = 512
    blk = lambda off: pl.BlockSpec((1, tc, SGU_WIDTH), lambda b, i: (b, i, off // SGU_WIDTH))
    return pl.pallas_call(
        functools.partial(_sgu_kernel, n_chunks=tc // SGU_CHUNK),
        out_shape=jax.ShapeDtypeStruct((bsz, seq, SGU_WIDTH), BF16),
        grid=(bsz, seq // tc),
        in_specs=[
            blk(OFF_U), blk(OFF_V),
            pl.BlockSpec((1, SGU_WIDTH), lambda b, i: (0, 0)),
            pl.BlockSpec((1, SGU_WIDTH), lambda b, i: (0, 0)),
            pl.BlockSpec((SGU_GROUPS, SGU_CHUNK, SGU_CHUNK), lambda b, i: (0, 0, 0)),
            pl.BlockSpec((SGU_CHUNK, SGU_GROUPS), lambda b, i: (0, 0)),
        ],
        out_specs=pl.BlockSpec((1, tc, SGU_WIDTH), lambda b, i: (b, i, 0)),
        compiler_params=_params(("parallel", "parallel")),
        name="sgu",
    )(proj, proj, ln_g, ln_b, w_s, b_s_t)


def _rotary(t, c, s1, s2):
    half = ROPE_DIM // 2
    return t * c + pltpu.roll(t, LANES - half, 1) * s1 + pltpu.roll(t, half, 1) * s2


def _rope_kernel(q_ref, k_ref, v_ref, c_ref, s1_ref, s2_ref, qt_ref, kr_ref, vt_ref, km_ref):
    c, s1, s2 = c_ref[...], s1_ref[...], s2_ref[...]
    for h in range(ATT_HEADS):
        cols = slice(h * ATT_HEAD_DIM, (h + 1) * ATT_HEAD_DIM)
        q = _rotary(q_ref[0, :, cols], c, s1, s2)
        k = _rotary(k_ref[0, :, cols], c, s1, s2)
        qt_ref[0, h, 0] = q.T
        kr_ref[0, h, 0] = k.astype(BF16)
        vt_ref[0, h, 0] = v_ref[0, :, cols].T.astype(BF16)
        km_ref[0, h, 0] = jnp.mean(k, axis=0, keepdims=True)


def _rope_prep(proj, tab_c, tab_s1, tab_s2):
    bsz, seq, _ = proj.shape
    nblk = seq // MOBA_BLOCK
    hd, nh = ATT_HEAD_DIM, ATT_HEADS
    width = nh * hd
    sec = lambda off: pl.BlockSpec((1, MOBA_BLOCK, width), lambda i, b: (b, i, off // width))
    tab = pl.BlockSpec((MOBA_BLOCK, hd), lambda i, b: (i, 0))
    out = lambda rows, cols: pl.BlockSpec((1, nh, 1, rows, cols), lambda i, b: (b, 0, i, 0, 0))
    return pl.pallas_call(
        _rope_kernel,
        out_shape=(
            jax.ShapeDtypeStruct((bsz, nh, nblk, hd, MOBA_BLOCK), F32),
            jax.ShapeDtypeStruct((bsz, nh, nblk, MOBA_BLOCK, hd), BF16),
            jax.ShapeDtypeStruct((bsz, nh, nblk, hd, MOBA_BLOCK), BF16),
            jax.ShapeDtypeStruct((bsz, nh, nblk, 1, hd), F32),
        ),
        grid=(nblk, bsz),
        in_specs=[sec(OFF_Q), sec(OFF_K), sec(OFF_VA), tab, tab, tab],
        out_specs=(out(hd, MOBA_BLOCK), out(MOBA_BLOCK, hd), out(hd, MOBA_BLOCK), out(1, hd)),
        compiler_params=_params(("parallel", "parallel")),
        name="rope_prep",
    )(proj, proj, proj, tab_c, tab_s1, tab_s2)


LOG2E = 1.4426950408889634


def _moba_kernel(qt_ref, kr_ref, vt_ref, km_ref, o_ref, sel_ref, qs_ref, acc_ref, *, nblk):
    i = pl.program_id(1)
    tq, nh, hd = MOBA_BLOCK, ATT_HEADS, ATT_HEAD_DIM
    blk = lax.broadcasted_iota(jnp.int32, (nblk, tq), 0)
    valid = blk < i
    krow = lax.broadcasted_iota(jnp.int32, (tq, tq), 0)
    qcol = lax.broadcasted_iota(jnp.int32, (tq, tq), 1)

    stats = []
    for h in range(nh):
        qt = qt_ref[0, h, 0]
        qs_ref[h] = (qt * (hd ** -0.5 * LOG2E)).astype(BF16)

        gate = jnp.dot(km_ref[0, h], qt, preferred_element_type=F32,
                       precision=lax.Precision.HIGHEST)
        gate = jnp.where(valid, gate, NEG_INF)
        cnt = jnp.zeros((nblk, tq), F32)
        for jp in range(nblk):
            gj = gate[jp:jp + 1, :]
            beats = (gj > gate) | ((gj == gate) & (blk > jp))
            cnt = cnt + jnp.where(beats, 1.0, 0.0)
        sel_ref[h] = jnp.where(valid & (cnt < MOBA_TOPK), 1.0, 0.0)

        s = jnp.dot(kr_ref[0, h, i], qs_ref[h], preferred_element_type=F32)
        s = jnp.where(krow <= qcol, s, NEG_INF)
        m0 = jnp.max(s, axis=0, keepdims=True)
        p = jnp.exp2(s - m0)
        acc_ref[h] = jnp.dot(vt_ref[0, h, i], p.astype(BF16), preferred_element_type=F32)
        stats += [m0, jnp.sum(p, axis=0, keepdims=True)]

    def scores(j):
        return [jnp.dot(kr_ref[0, h, j], qs_ref[h], preferred_element_type=F32) for h in range(nh)]

    def body(j, carry):
        s_next = scores(jnp.minimum(j + 1, nblk - 1))
        out = []
        for h in range(nh):
            m, l = carry[2 * h], carry[2 * h + 1]
            s = jnp.where(sel_ref[h, pl.ds(j, 1), :] > 0.5, carry[2 * nh + h], NEG_INF)
            m_new = jnp.maximum(m, jnp.max(s, axis=0, keepdims=True))
            alpha = jnp.exp2(m - m_new)
            p = jnp.exp2(s - m_new)
            acc_ref[h] = alpha * acc_ref[h] + jnp.dot(vt_ref[0, h, j], p.astype(BF16),
                                                      preferred_element_type=F32)
            out += [m_new, alpha * l + jnp.sum(p, axis=0, keepdims=True)]
        return tuple(out + s_next)

    stats = lax.fori_loop(0, i, body, tuple(stats + scores(0)))
    for h in range(nh):
        o_ref[0, :, h * hd:(h + 1) * hd] = (acc_ref[h] / stats[2 * h + 1]).T.astype(BF16)


def _moba(qt, kr, vt, km):
    bsz, nh, nblk, hd, tq = qt.shape
    seq = nblk * tq
    full = lambda shape: pl.BlockSpec((1, nh) + shape, lambda b, i: (b, 0) + (0,) * len(shape))
    return pl.pallas_call(
        functools.partial(_moba_kernel, nblk=nblk),
        out_shape=jax.ShapeDtypeStruct((bsz, seq, nh * hd), BF16),
        grid=(bsz, nblk),
        in_specs=[
            pl.BlockSpec((1, nh, 1, hd, tq), lambda b, i: (b, 0, i, 0, 0)),
            full((nblk, tq, hd)), full((nblk, hd, tq)), full((nblk, hd)),
        ],
        out_specs=pl.BlockSpec((1, tq, nh * hd), lambda b, i: (b, i, 0)),
        scratch_shapes=[pltpu.VMEM((nh, nblk, tq), F32), pltpu.VMEM((nh, hd, tq), BF16),
                        pltpu.VMEM((nh, hd, tq), F32)],
        compiler_params=_params(("parallel", "arbitrary")),
        name="moba",
    )(qt, kr, vt, km)


def _softplus(x):
    return jnp.maximum(x, 0.0) + jnp.log1p(jnp.exp(-jnp.abs(x)))


def _ssd_kernel(xbc_ref, z_ref, dt_ref, cw_ref, cb_ref, dtb_ref, alog_ref, dsk_ref, nw_ref,
                o_ref, prev_ref, state_ref, y_ref):
    L, P, N = SSM_CHUNK, SSM_HEAD_DIM, SSM_STATE
    hi = lax.Precision.HIGHEST

    @pl.when(pl.program_id(1) == 0)
    def _():
        prev_ref[...] = jnp.zeros_like(prev_ref)
        state_ref[...] = jnp.zeros_like(state_ref)

    cur = xbc_ref[0]
    prev = prev_ref[...]
    crow = lax.broadcasted_iota(jnp.int32, cur.shape, 0)
    acc = cur * cw_ref[SSM_CONV - 1:SSM_CONV, :] + cb_ref[...]
    for j in range(1, SSM_CONV):
        shifted = pltpu.roll(jnp.where(crow >= L - j, prev, cur), j, 0)
        acc = acc + shifted * cw_ref[SSM_CONV - 1 - j:SSM_CONV - j, :]
    prev_ref[...] = cur
    xc = _silu(acc)
    xs = xc[:, :SSM_WIDTH]
    bm = xc[:, SSM_WIDTH:SSM_WIDTH + SSM_GROUPS * N].astype(BF16)
    cm = xc[:, SSM_WIDTH + SSM_GROUPS * N:].astype(BF16)
    xs_b = xs.astype(BF16)
    xs_t = xs.T

    dt = _softplus(dt_ref[0] + dtb_ref[...])
    da = dt * (-jnp.exp(alog_ref[...]))
    row = lax.broadcasted_iota(jnp.int32, (L, L), 0)
    col = lax.broadcasted_iota(jnp.int32, (L, L), 1)
    causal = row >= col
    acum = jnp.dot(jnp.where(causal, 1.0, 0.0), da, preferred_element_type=F32, precision=hi)
    acum_t = jnp.dot(da.T, jnp.where(row <= col, 1.0, 0.0), preferred_element_type=F32, precision=hi)
    dt_t = dt.T
    last_t = acum_t[:, L - 1:L]
    w_end_t = jnp.exp(last_t - acum_t) * dt_t
    e_last_t = jnp.exp(last_t)
    e_acum = jnp.exp(acum)

    for g in range(SSM_GROUPS):
        b_g = bm[:, g * N:(g + 1) * N]
        c_g = cm[:, g * N:(g + 1) * N]
        cb = lax.dot_general(c_g, b_g, (((1,), (1,)), ((), ())), preferred_element_type=F32)
        for hh in range(SSM_HEADS // SSM_GROUPS):
            h = g * (SSM_HEADS // SSM_GROUPS) + hh
            hs = slice(h * P, (h + 1) * P)
            decay = jnp.exp(jnp.where(causal, acum[:, h:h + 1] - acum_t[h:h + 1, :], -jnp.inf))
            mat = (cb * decay * dt_t[h:h + 1, :]).astype(BF16)
            y = jnp.dot(mat, xs_b[:, hs], preferred_element_type=F32)
            st = state_ref[hs, :]
            y = y + lax.dot_general(c_g, st.astype(BF16), (((1,), (1,)), ((), ())),
                                    preferred_element_type=F32) * e_acum[:, h:h + 1]
            y_ref[:, hs] = y
            xw = (xs_t[hs, :] * w_end_t[h:h + 1, :]).astype(BF16)
            state_ref[hs, :] = st * e_last_t[h:h + 1, :] + jnp.dot(xw, b_g, preferred_element_type=F32)

    y = y_ref[...] + dsk_ref[...] * xs
    y = y * _silu(z_ref[0])
    gw = SSM_WIDTH // SSM_GROUPS
    for g in range(SSM_GROUPS):
        cols = slice(g * gw, (g + 1) * gw)
        yg = y[:, cols]
        yg = yg * lax.rsqrt(jnp.mean(yg * yg, axis=-1, keepdims=True) + NORM_EPS)
        o_ref[0, :, cols] = (yg * nw_ref[:, cols]).astype(BF16)


def _ssd(proj, dt_raw, conv_w, conv_b, dt_bias, a_log, d_skip, ssm_norm):
    bsz, seq, _ = proj.shape
    L = SSM_CHUNK
    vec = lambda n: pl.BlockSpec((1, n), lambda b, c: (0, 0))
    return pl.pallas_call(
        _ssd_kernel,
        out_shape=jax.ShapeDtypeStruct((bsz, seq, SSM_WIDTH), BF16),
        grid=(bsz, seq // L),
        in_specs=[
            pl.BlockSpec((1, L, SSM_CONV_DIM), lambda b, c: (b, c, 0)),
            pl.BlockSpec((1, L, SSM_WIDTH), lambda b, c: (b, c, OFF_Z // SSM_WIDTH)),
            pl.BlockSpec((1, L, LANES), lambda b, c: (b, c, 0)),
            pl.BlockSpec((SSM_CONV, SSM_CONV_DIM), lambda b, c: (0, 0)),
            vec(SSM_CONV_DIM), vec(LANES), vec(LANES), vec(SSM_WIDTH), vec(SSM_WIDTH),
        ],
        out_specs=pl.BlockSpec((1, L, SSM_WIDTH), lambda b, c: (b, c, 0)),
        scratch_shapes=[
            pltpu.VMEM((L, SSM_CONV_DIM), F32),
            pltpu.VMEM((SSM_WIDTH, SSM_STATE), F32),
            pltpu.VMEM((L, SSM_WIDTH), F32),
        ],
        compiler_params=_params(("parallel", "arbitrary")),
        name="ssd",
    )(proj, proj, dt_raw, conv_w, conv_b, dt_bias, a_log, d_skip, ssm_norm)


def _route(logits):
    lane = lax.broadcasted_iota(jnp.int32, logits.shape, 1).astype(F32)
    big = float(LANES)

    def first_argmax(vals, mask):
        top = jnp.max(vals, axis=-1, keepdims=True)
        idx = jnp.min(jnp.where(mask & (vals >= top), lane, big), axis=-1, keepdims=True)
        return top, idx

    def masked_softmax(mask):
        z = jnp.where(mask, logits, -jnp.inf)
        e = jnp.exp(z - jnp.max(z, axis=-1, keepdims=True))
        return e / jnp.sum(e, axis=-1, keepdims=True)

    gmask = lane < MOE_GROUPS
    g_prob, g_idx = first_argmax(jnp.where(gmask, masked_softmax(gmask), -1.0), gmask)
    lo = MOE_GROUPS + g_idx * MOE_EPG
    fmask = (lane >= lo) & (lane < lo + MOE_EPG)
    p_exp = jnp.where(fmask, masked_softmax(fmask), -1.0)
    p1, i1 = first_argmax(p_exp, fmask)
    rest = fmask & (lane != i1)
    p2, i2 = first_argmax(jnp.where(rest, p_exp, -1.0), rest)
    den = p1 + p2
    out = jnp.where(lane == 0, i1 - MOE_GROUPS, 0.0)
    out = jnp.where(lane == 1, i2 - MOE_GROUPS, out)
    out = jnp.where(lane == 2, g_prob * p1 / den, out)
    out = jnp.where(lane == 3, g_prob * p2 / den, out)
    return out


def _outproj_kernel(ya_ref, yb_ref, yc_ref, x_ref, g1_ref, wa_ref, wb_ref, wc_ref,
                    n2_ref, sc_ref, sh_ref, wr_ref, br_ref, xo_ref, hc_ref, rt_ref):
    mix = jnp.dot(ya_ref[0], wa_ref[...], preferred_element_type=F32)
    mix = mix + jnp.dot(yb_ref[0], wb_ref[...], preferred_element_type=F32)
    mix = mix + jnp.dot(yc_ref[0], wc_ref[...], preferred_element_type=F32)
    xn = x_ref[0] + g1_ref[0] * mix
    xo_ref[0] = xn
    h = _modulated_norm(xn, n2_ref[...], sc_ref[0], sh_ref[0])
    half = N_PACK * LANES
    packed = _pack_halves(h[:, :half], h[:, half:])
    for c in range(N_PACK):
        hc_ref[0, c] = packed[:, c * LANES:(c + 1) * LANES]
    h_hi = h.astype(BF16)
    h_lo = (h - h_hi.astype(F32)).astype(BF16)
    part = jnp.dot(h_hi, wr_ref[...], preferred_element_type=F32)
    logits = (part[:, :LANES] + part[:, LANES:]
              + jnp.dot(h_lo, wr_ref[:, :LANES], preferred_element_type=F32) + br_ref[...])
    rt_ref[0] = _route(logits)


TM_TOK = 512


def _out_proj_route(ya, yb, yc, x, g1, wa, wb, wc, n2, sc2, sh2, w_r, b_r):
    bsz, seq, d = x.shape
    tm = TM_TOK
    nt = seq // tm
    row = lambda w: pl.BlockSpec((1, tm, w), lambda b, i: (b, i, 0))
    per_b = pl.BlockSpec((1, 1, d), lambda b, i: (b, 0, 0))
    const = lambda a: pl.BlockSpec(a.shape, lambda b, i: (0,) * a.ndim,
                                   pipeline_mode=pl.Buffered(1))
    return pl.pallas_call(
        _outproj_kernel,
        out_shape=(jax.ShapeDtypeStruct((bsz, seq, d), F32),
                   jax.ShapeDtypeStruct((bsz * nt, N_PACK, tm, LANES), U32),
                   jax.ShapeDtypeStruct((bsz, seq, LANES), F32)),
        grid=(bsz, nt),
        in_specs=[row(ya.shape[-1]), row(yb.shape[-1]), row(yc.shape[-1]), row(d), per_b,
                  const(wa), const(wb), const(wc), const(n2), per_b, per_b, const(w_r), const(b_r)],
        out_specs=(row(d),
                   pl.BlockSpec((1, N_PACK, tm, LANES), lambda b, i: (b * nt + i, 0, 0, 0)),
                   row(LANES)),
        compiler_params=_params(("parallel", "parallel")),
        name="out_proj_route",
    )(ya, yb, yc, x, g1, wa, wb, wc, n2, sc2, sh2, w_r, b_r)


BLOCK_SHIFT = MOE_BLOCK.bit_length() - 1


TOK_SHIFT = TM_TOK.bit_length() - 1
GATHER_UNROLL = 8
DMA_QUEUES = 2
MXU_WIDTH = 256


def _expert_kernel(be_ref, tok_ref, h_hbm, wg_ref, wu_ref, wd_ref, o_ref,
                   buf_ref, x_ref, wgb_ref, wub_ref, wdb_ref, sem):
    i = pl.program_id(0)
    last = pl.num_programs(0) - 1
    slot = i % 2
    nxt = jnp.minimum(i + 1, last)
    half = N_PACK * LANES

    def gather(blk, slot_, mi):
        tok = tok_ref[blk * MOE_BLOCK + mi]
        return pltpu.make_async_copy(
            h_hbm.at[jnp.right_shift(tok, TOK_SHIFT), :, jnp.bitwise_and(tok, TM_TOK - 1), :],
            buf_ref.at[slot_, :, mi, :], sem.at[slot_])

    def for_rows(fn):
        def body(r, carry):
            for u in range(GATHER_UNROLL):
                fn(r * GATHER_UNROLL + u)
            return carry
        lax.fori_loop(0, MOE_BLOCK // GATHER_UNROLL, body, 0)

    @pl.when(i == 0)
    def _():
        for_rows(lambda mi: gather(0, 0, mi).start())

    @pl.when((i == 0) | (be_ref[i] != be_ref[jnp.maximum(i - 1, 0)]))
    def _():
        wgb_ref[...] = wg_ref[0, 0].astype(BF16)
        wub_ref[...] = wu_ref[0, 0].astype(BF16)
        wdb_ref[...] = wd_ref[0, 0].astype(BF16)

    for_rows(lambda mi: gather(i, slot, mi).wait())
    for c in range(N_PACK):
        lo, hi = _unpack_halves(buf_ref[slot, c])
        x_ref[:, c * LANES:(c + 1) * LANES] = lo.astype(BF16)
        x_ref[:, half + c * LANES:half + (c + 1) * LANES] = hi.astype(BF16)

    ff = wgb_ref.shape[1]
    n_up, n_down = ff // MXU_WIDTH, half // MXU_WIDTH
    rows_per_piece = MOE_BLOCK // (2 * n_up + n_down)
    pieces = iter(range(2 * n_up + n_down))

    def issue_piece():
        p = next(pieces)
        for mi in range(p * rows_per_piece, (p + 1) * rows_per_piece):
            gather(nxt, 1 - slot, mi).start(priority=mi % DMA_QUEUES)

    hid = []
    for p in range(n_up):
        cols = slice(p * MXU_WIDTH, (p + 1) * MXU_WIDTH)
        issue_piece()
        gate = jnp.dot(x_ref[...], wgb_ref[:, cols], preferred_element_type=F32)
        issue_piece()
        up = jnp.dot(x_ref[...], wub_ref[:, cols], preferred_element_type=F32)
        hid.append((_silu(gate) * up).astype(BF16))
    hid = jnp.concatenate(hid, axis=-1)
    for p in range(n_down):
        cols = slice(p * MXU_WIDTH, (p + 1) * MXU_WIDTH)
        issue_piece()
        y_lo = jnp.dot(hid, wdb_ref[:, cols], preferred_element_type=F32)
        y_hi = jnp.dot(hid, wdb_ref[:, half + p * MXU_WIDTH:half + (p + 1) * MXU_WIDTH],
                       preferred_element_type=F32)
        packed = _pack_halves(y_lo, y_hi)
        for c in range(MXU_WIDTH // LANES):
            o_ref[0, p * (MXU_WIDTH // LANES) + c] = packed[:, c * LANES:(c + 1) * LANES]

    @pl.when(i == last)
    def _():
        for_rows(lambda mi: gather(nxt, 1 - slot, mi).wait())


def _experts(layer, blk_exp, row_tok, h_chunks, w_gate, w_up, w_down):
    n_blocks = blk_exp.shape[0]
    d, ff = w_gate.shape[2], w_gate.shape[3]
    return pl.pallas_call(
        _expert_kernel,
        out_shape=jax.ShapeDtypeStruct((n_blocks, N_PACK, MOE_BLOCK, LANES), U32),
        grid_spec=pltpu.PrefetchScalarGridSpec(
            num_scalar_prefetch=2,
            grid=(n_blocks,),
            in_specs=[pl.BlockSpec(memory_space=pl.ANY)] + [
                pl.BlockSpec(shape, lambda i, be, rt: (layer, be[i], 0, 0))
                for shape in ((1, 1, d, ff), (1, 1, d, ff), (1, 1, ff, d))],
            out_specs=pl.BlockSpec((1, N_PACK, MOE_BLOCK, LANES), lambda i, be, rt: (i, 0, 0, 0)),
            scratch_shapes=[pltpu.VMEM((2, N_PACK, MOE_BLOCK, LANES), U32),
                            pltpu.VMEM((MOE_BLOCK, d), BF16),
                            pltpu.VMEM((d, ff), BF16), pltpu.VMEM((d, ff), BF16),
                            pltpu.VMEM((ff, d), BF16), pltpu.SemaphoreType.DMA((2,))],
        ),
        compiler_params=_params(("arbitrary",)),
        name="moe_experts",
    )(blk_exp, row_tok, h_chunks, w_gate, w_up, w_down)


def _combine_kernel(dest_ref, y_hbm, x_ref, g2_ref, rt_ref, fn_ref, o_ref, buf_ref, sem, *, final):
    i = pl.program_id(0)
    tm = TM_TOK

    def gather(k, mi):
        d = dest_ref[(i * tm + mi) * 2 + k]
        return pltpu.make_async_copy(
            y_hbm.at[jnp.right_shift(d, BLOCK_SHIFT), :, jnp.bitwise_and(d, MOE_BLOCK - 1), :],
            buf_ref.at[k, :, mi, :], sem)

    def issue(mi, carry):
        for k in range(2):
            gather(k, mi).start(priority=k % DMA_QUEUES)
        return carry

    def drain(mi, carry):
        for k in range(2):
            gather(k, mi).wait()
        return carry

    lax.fori_loop(0, tm, issue, 0)
    lax.fori_loop(0, tm, drain, 0)
    rt = rt_ref[...]
    gate0, gate1 = rt[:, 2:3], rt[:, 3:4]
    for c in range(N_PACK):
        halves = zip(_unpack_halves(buf_ref[0, c]), _unpack_halves(buf_ref[1, c]))
        for part, (y0, y1) in enumerate(halves):
            cols = slice((part * N_PACK + c) * LANES, (part * N_PACK + c + 1) * LANES)
            o_ref[:, cols] = x_ref[:, cols] + g2_ref[0, :, cols] * (y0 * gate0 + y1 * gate1)
    if final:
        x = o_ref[...]
        ms = jnp.mean(x * x, axis=-1, keepdims=True)
        o_ref[...] = (x * lax.rsqrt(ms + NORM_EPS)) * fn_ref[...]


def _combine(dest_flat, y_rows, x2d, g2, route2d, seq, final_g, final):
    n_tok, d = x2d.shape
    tm = TM_TOK
    per_seq = seq // tm
    return pl.pallas_call(
        functools.partial(_combine_kernel, final=final),
        out_shape=jax.ShapeDtypeStruct((n_tok, d), F32),
        grid_spec=pltpu.PrefetchScalarGridSpec(
            num_scalar_prefetch=1,
            grid=(n_tok // tm,),
            in_specs=[pl.BlockSpec(memory_space=pl.ANY),
                      pl.BlockSpec((tm, d), lambda i, dr: (i, 0)),
                      pl.BlockSpec((1, 1, d), lambda i, dr: (i // per_seq, 0, 0)),
                      pl.BlockSpec((tm, LANES), lambda i, dr: (i, 0)),
                      pl.BlockSpec((1, d), lambda i, dr: (0, 0))],
            out_specs=pl.BlockSpec((tm, d), lambda i, dr: (i, 0)),
            scratch_shapes=[pltpu.VMEM((2, N_PACK, tm, LANES), U32), pltpu.SemaphoreType.DMA],
        ),
        compiler_params=_params(("arbitrary",)),
        name="moe_combine",
    )(dest_flat, y_rows, x2d, g2, route2d, final_g)


def _dispatch_plan(route2d):
    n_tok = route2d.shape[0]
    n_asg = n_tok * 2
    flat_e = route2d[:, 0:2].astype(jnp.int32).reshape(n_asg)
    asg = jnp.arange(n_asg, dtype=jnp.int32)
    s_e, order = lax.sort((flat_e, asg), num_keys=1)
    ids = jnp.arange(MOE_EXPERTS, dtype=jnp.int32)
    start = jnp.sum(s_e[:, None] < ids[None, :], axis=0, dtype=jnp.int32)
    counts = jnp.sum(s_e[:, None] == ids[None, :], axis=0, dtype=jnp.int32)
    pcounts = (counts + MOE_BLOCK - 1) // MOE_BLOCK * MOE_BLOCK
    pend = jnp.cumsum(pcounts)
    pstart = pend - pcounts
    dest_sorted = pstart[s_e] + asg - start[s_e]
    _, dest = lax.sort((order, dest_sorted), num_keys=1)
    n_blocks = n_asg // MOE_BLOCK + MOE_EXPERTS
    blk_start = jnp.arange(n_blocks, dtype=jnp.int32) * MOE_BLOCK
    blk_exp = jnp.minimum(jnp.sum(pend[None, :] <= blk_start[:, None], axis=1, dtype=jnp.int32),
                          MOE_EXPERTS - 1)
    row = jnp.arange(n_blocks * MOE_BLOCK, dtype=jnp.int32)
    row_e = jnp.repeat(blk_exp, MOE_BLOCK)
    off = row - pstart[row_e]
    src = jnp.minimum(start[row_e] + off, n_asg - 1)
    row_tok = jnp.where(off < counts[row_e], order[src] // 2, 0).astype(jnp.int32)
    return dest, row_tok, blk_exp


def _rotary_tables(seq):
    pos = jnp.arange(seq, dtype=F32)
    inv_freq = ROPE_THETA ** (-jnp.arange(0, ROPE_DIM, 2, dtype=F32) / ROPE_DIM)
    ang = pos[:, None] * inv_freq[None, :]
    cos, sin = jnp.cos(ang), jnp.sin(ang)
    half = ROPE_DIM // 2
    ones = jnp.ones((seq, LANES - ROPE_DIM), F32)
    tab_c = jnp.concatenate([cos, cos, ones], axis=-1)
    tab_s1 = jnp.concatenate([-sin, jnp.zeros((seq, LANES - half), F32)], axis=-1)
    tab_s2 = jnp.concatenate([jnp.zeros((seq, half), F32), sin,
                              jnp.zeros((seq, LANES - ROPE_DIM), F32)], axis=-1)
    return tab_c, tab_s1, tab_s2


def _pad_lanes(v):
    return jnp.pad(v, (0, LANES - v.shape[0])).reshape(1, LANES)


def kernel(x, c, norm1, norm2, w_ada, b_ada, w_in, w_out, sgu_ln_g, sgu_ln_b, sgu_w, sgu_b, conv_w, conv_b, dt_bias, a_log, d_skip, ssm_norm, w_coarse, b_coarse, w_fine, b_fine, w_gate, w_up, w_down, final_norm):
    bsz, seq, d = x.shape
    depth = w_in.shape[0]
    n_tok = bsz * seq

    c_pad = jnp.pad(c, ((0, 8 - bsz % 8 if bsz % 8 else 0), (0, 0)))
    mod = _ada_mod(c_pad, w_ada, b_ada)[:, :bsz].reshape(depth, bsz, 6, 1, d)
    tab_c, tab_s1, tab_s2 = _rotary_tables(seq)

    for l in range(depth):
        sh1, sc1, g1, sh2, sc2, g2 = [mod[l, :, i] for i in range(6)]
        wl = w_in[l]
        w_main = jnp.concatenate(
            [wl[:, 3584:5120], wl[:, 0:512], wl[:, 2560:3584], wl[:, 512:1024], wl[:, 1024:2560]],
            axis=1).astype(BF16)
        w_dt = jnp.pad(wl[:, 5120:], ((0, 0), (0, LANES - SSM_HEADS))).astype(BF16)
        proj, dt_raw = _in_proj(x, norm1[l].reshape(1, d), sc1, sh1, w_main, w_dt)

        ya = _sgu(proj, sgu_ln_g[l].reshape(1, -1), sgu_ln_b[l].reshape(1, -1), sgu_w[l], sgu_b[l].T)
        qt, kr, vt, km = _rope_prep(proj, tab_c, tab_s1, tab_s2)
        yb = _moba(qt, kr, vt, km.reshape(km.shape[:3] + (ATT_HEAD_DIM,)))
        yc = _ssd(proj, dt_raw, conv_w[l], conv_b[l].reshape(1, -1), _pad_lanes(dt_bias[l]),
                  _pad_lanes(a_log[l]), jnp.repeat(d_skip[l], SSM_HEAD_DIM).reshape(1, -1),
                  ssm_norm[l].reshape(1, -1))

        wo = w_out[l].astype(BF16)
        w_r = jnp.concatenate([w_coarse[l], w_fine[l].transpose(1, 0, 2).reshape(d, MOE_EXPERTS)], axis=1)
        w_r = jnp.pad(w_r, ((0, 0), (0, LANES - w_r.shape[1])))
        w_r_hi = w_r.astype(BF16)
        w_r = jnp.concatenate([w_r_hi, (w_r - w_r_hi.astype(F32)).astype(BF16)], axis=1)
        b_r = _pad_lanes(jnp.concatenate([b_coarse[l], b_fine[l].reshape(-1)]))
        x, h_chunks, route = _out_proj_route(
            ya, yb, yc, x, g1, wo[:SGU_WIDTH], wo[SGU_WIDTH:2 * SGU_WIDTH], wo[2 * SGU_WIDTH:],
            norm2[l].reshape(1, d), sc2, sh2, w_r, b_r)

        route2d = route.reshape(n_tok, LANES)
        dest, row_tok, blk_exp = _dispatch_plan(route2d)
        y_rows = _experts(l, blk_exp, row_tok, h_chunks, w_gate, w_up, w_down)
        x = _combine(dest, y_rows, x.reshape(n_tok, d), g2, route2d, seq,
                     final_norm.reshape(1, d), final=(l == depth - 1)).reshape(bsz, seq, d)

    return x
```

```python
import functools

import jax
import jax.numpy as jnp
from jax import lax
from jax.experimental import pallas as pl
from jax.experimental.pallas import tpu as pltpu

F32 = jnp.float32
BF16 = jnp.bfloat16

LANES = 128
NORM_EPS = 1e-6
NEG_INF = -1e30

D_MODEL = 2048
N_CHUNK = D_MODEL // LANES
N_PACK = N_CHUNK // 2
SGU_WIDTH = 512
SGU_GROUPS = 4
SGU_CHUNK = 128
ATT_HEADS = 4
ATT_HEAD_DIM = 128
ROPE_DIM = 32
ROPE_THETA = 500000.0
MOBA_BLOCK = 256
MOBA_TOPK = 3
SSM_WIDTH = 1024
SSM_HEAD_DIM = 64
SSM_HEADS = 16
SSM_GROUPS = 2
SSM_STATE = 128
SSM_CONV = 4
SSM_CHUNK = 128
SSM_CONV_DIM = 1536
MOE_GROUPS = 4
MOE_EPG = 8
MOE_EXPERTS = 32
MOE_FF = 512
MOE_BLOCK = 512

PROJ_MAIN = 5120
OFF_U, OFF_Z, OFF_V, OFF_Q, OFF_K, OFF_VA = 1536, 2048, 3072, 3584, 4096, 4608

VMEM_LIMIT = 56 * 1024 * 1024


def _params(sem, vmem=VMEM_LIMIT):
    return pltpu.CompilerParams(dimension_semantics=sem, vmem_limit_bytes=vmem)


def _silu(x):
    return x * jax.nn.sigmoid(x)


U32 = jnp.uint32
HI_MASK = 0xFFFF0000


def _pack_halves(lo, hi):
    lo_bits = lax.bitcast_convert_type(lo.astype(BF16).astype(F32), U32)
    hi_bits = lax.bitcast_convert_type(hi.astype(BF16).astype(F32), U32)
    return jnp.bitwise_or(jnp.bitwise_and(hi_bits, U32(HI_MASK)), jnp.right_shift(lo_bits, U32(16)))


def _unpack_halves(w):
    lo = lax.bitcast_convert_type(jnp.left_shift(w, U32(16)), F32)
    hi = lax.bitcast_convert_type(jnp.bitwise_and(w, U32(HI_MASK)), F32)
    return lo, hi


def _ada_kernel(c_ref, w_ref, b_ref, o_ref):
    ca = _silu(c_ref[...]).astype(BF16)
    o_ref[0] = jnp.dot(ca, w_ref[0].astype(BF16), preferred_element_type=F32) + b_ref[0]


def _ada_mod(c_pad, w_ada, b_ada):
    depth, d, n = w_ada.shape
    tn = 1024
    return pl.pallas_call(
        _ada_kernel,
        out_shape=jax.ShapeDtypeStruct((depth, c_pad.shape[0], n), F32),
        grid=(depth, n // tn),
        in_specs=[
            pl.BlockSpec(c_pad.shape, lambda l, j: (0, 0)),
            pl.BlockSpec((1, d, tn), lambda l, j: (l, 0, j)),
            pl.BlockSpec((1, 1, tn), lambda l, j: (l, 0, j)),
        ],
        out_specs=pl.BlockSpec((1, c_pad.shape[0], tn), lambda l, j: (l, 0, j)),
        compiler_params=_params(("parallel", "parallel")),
        name="ada_mod",
    )(c_pad, w_ada, b_ada.reshape(depth, 1, n))


def _modulated_norm(x, g, sc, sh):
    ms = jnp.mean(x * x, axis=-1, keepdims=True)
    return (x * lax.rsqrt(ms + NORM_EPS)) * g * (1.0 + sc) + sh


def _inproj_kernel(x_ref, g_ref, sc_ref, sh_ref, w_ref, wdt_ref, o_ref, dt_ref, h_ref):
    @pl.when(pl.program_id(2) == 0)
    def _():
        h = _modulated_norm(x_ref[0], g_ref[...], sc_ref[0], sh_ref[0]).astype(BF16)
        h_ref[...] = h
        dt_ref[0] = jnp.dot(h, wdt_ref[...], preferred_element_type=F32)

    o_ref[0] = jnp.dot(h_ref[...], w_ref[...], preferred_element_type=F32)


def _in_proj(x, g, sc, sh, w, wdt):
    bsz, seq, d = x.shape
    n = w.shape[1]
    tm, tn = 512, 1024
    return pl.pallas_call(
        _inproj_kernel,
        out_shape=(jax.ShapeDtypeStruct((bsz, seq, n), F32),
                   jax.ShapeDtypeStruct((bsz, seq, LANES), F32)),
        grid=(bsz, seq // tm, n // tn),
        in_specs=[
            pl.BlockSpec((1, tm, d), lambda b, i, j: (b, i, 0)),
            pl.BlockSpec((1, d), lambda b, i, j: (0, 0)),
            pl.BlockSpec((1, 1, d), lambda b, i, j: (b, 0, 0)),
            pl.BlockSpec((1, 1, d), lambda b, i, j: (b, 0, 0)),
            pl.BlockSpec((d, tn), lambda b, i, j: (0, j)),
            pl.BlockSpec((d, LANES), lambda b, i, j: (0, 0)),
        ],
        out_specs=(pl.BlockSpec((1, tm, tn), lambda b, i, j: (b, i, j)),
                   pl.BlockSpec((1, tm, LANES), lambda b, i, j: (b, i, 0))),
        scratch_shapes=[pltpu.VMEM((tm, d), BF16)],
        compiler_params=_params(("parallel", "parallel", "arbitrary")),
        name="in_proj",
    )(x, g, sc, sh, w, wdt)


def _sgu_kernel(u_ref, v_ref, g_ref, b_ref, w_ref, bs_ref, o_ref, *, n_chunks):
    row = lax.broadcasted_iota(jnp.int32, (SGU_CHUNK, SGU_CHUNK), 0)
    col = lax.broadcasted_iota(jnp.int32, (SGU_CHUNK, SGU_CHUNK), 1)
    w_causal = [jnp.where(row >= col, w_ref[g], 0.0).astype(BF16) for g in range(SGU_GROUPS)]
    for ci in range(n_chunks):
        rows = slice(ci * SGU_CHUNK, (ci + 1) * SGU_CHUNK)
        u = jax.nn.gelu(u_ref[0, rows, :])
        vf = jax.nn.gelu(v_ref[0, rows, :])
        mu = jnp.mean(vf, axis=-1, keepdims=True)
        var = jnp.mean(jnp.square(vf - mu), axis=-1, keepdims=True)
        vn = ((vf - mu) * lax.rsqrt(var + NORM_EPS) * g_ref[...] + b_ref[...]).astype(BF16)
        for g in range(SGU_GROUPS):
            cols = slice(g * LANES, (g + 1) * LANES)
            s = jnp.dot(w_causal[g], vn[:, cols], preferred_element_type=F32) + bs_ref[:, g:g + 1]
            o_ref[0, rows, cols] = (u[:, cols] * s).astype(BF16)


def _sgu(proj, ln_g, ln_b, w_s, b_s_t):
    bsz, seq, _ = proj.shape
    tc = 512
    blk = lambda off: pl.BlockSpec((1, tc, SGU_WIDTH), lambda b, i: (b, i, off // SGU_WIDTH))
    return pl.pallas_call(
        functools.partial(_sgu_kernel, n_chunks=tc // SGU_CHUNK),
        out_shape=jax.ShapeDtypeStruct((bsz, seq, SGU_WIDTH), BF16),
        grid=(bsz, seq // tc),
        in_specs=[
            blk(OFF_U), blk(OFF_V),
            pl.BlockSpec((1, SGU_WIDTH), lambda b, i: (0, 0)),
            pl.BlockSpec((1, SGU_WIDTH), lambda b, i: (0, 0)),
            pl.BlockSpec((SGU_GROUPS, SGU_CHUNK, SGU_CHUNK), lambda b, i: (0, 0, 0)),
            pl.BlockSpec((SGU_CHUNK, SGU_GROUPS), lambda b, i: (0, 0)),
        ],
        out_specs=pl.BlockSpec((1, tc, SGU_WIDTH), lambda b, i: (b, i, 0)),
        compiler_params=_params(("parallel", "parallel")),
        name="sgu",
    )(proj, proj, ln_g, ln_b, w_s, b_s_t)


def _rotary(t, c, s1, s2):
    half = ROPE_DIM // 2
    return t * c + pltpu.roll(t, LANES - half, 1) * s1 + pltpu.roll(t, half, 1) * s2


def _rope_kernel(q_ref, k_ref, v_ref, c_ref, s1_ref, s2_ref, qt_ref, kr_ref, vt_ref, km_ref):
    c, s1, s2 = c_ref[...], s1_ref[...], s2_ref[...]
    for h in range(ATT_HEADS):
        cols = slice(h * ATT_HEAD_DIM, (h + 1) * ATT_HEAD_DIM)
        q = _rotary(q_ref[0, :, cols], c, s1, s2)
        k = _rotary(k_ref[0, :, cols], c, s1, s2)
        qt_ref[0, h, 0] = q.T
        kr_ref[0, h, 0] = k.astype(BF16)
        vt_ref[0, h, 0] = v_ref[0, :, cols].T.astype(BF16)
        km_ref[0, h, 0] = jnp.mean(k, axis=0, keepdims=True)


def _rope_prep(proj, tab_c, tab_s1, tab_s2):
    bsz, seq, _ = proj.shape
    nblk = seq // MOBA_BLOCK
    hd, nh = ATT_HEAD_DIM, ATT_HEADS
    width = nh * hd
    sec = lambda off: pl.BlockSpec((1, MOBA_BLOCK, width), lambda i, b: (b, i, off // width))
    tab = pl.BlockSpec((MOBA_BLOCK, hd), lambda i, b: (i, 0))
    out = lambda rows, cols: pl.BlockSpec((1, nh, 1, rows, cols), lambda i, b: (b, 0, i, 0, 0))
    return pl.pallas_call(
        _rope_kernel,
        out_shape=(
            jax.ShapeDtypeStruct((bsz, nh, nblk, hd, MOBA_BLOCK), F32),
            jax.ShapeDtypeStruct((bsz, nh, nblk, MOBA_BLOCK, hd), BF16),
            jax.ShapeDtypeStruct((bsz, nh, nblk, hd, MOBA_BLOCK), BF16),
            jax.ShapeDtypeStruct((bsz, nh, nblk, 1, hd), F32),
        ),
        grid=(nblk, bsz),
        in_specs=[sec(OFF_Q), sec(OFF_K), sec(OFF_VA), tab, tab, tab],
        out_specs=(out(hd, MOBA_BLOCK), out(MOBA_BLOCK, hd), out(hd, MOBA_BLOCK), out(1, hd)),
        compiler_params=_params(("parallel", "parallel")),
        name="rope_prep",
    )(proj, proj, proj, tab_c, tab_s1, tab_s2)


LOG2E = 1.4426950408889634
SOFTMAX_ROWS = 64


def _moba_kernel(qt_ref, kr_ref, vt_ref, km_ref, o_ref, sel_ref, qs_ref, acc_ref, p_ref, s_ref, *, nblk):
    i = pl.program_id(1)
    tq, nh, hd = MOBA_BLOCK, ATT_HEADS, ATT_HEAD_DIM
    blk = lax.broadcasted_iota(jnp.int32, (nblk, tq), 0)
    valid = blk < i
    krow = lax.broadcasted_iota(jnp.int32, (tq, tq), 0)
    qcol = lax.broadcasted_iota(jnp.int32, (tq, tq), 1)

    stats = []
    for h in range(nh):
        qt = qt_ref[0, h, 0]
        qs_ref[h] = (qt * (hd ** -0.5 * LOG2E)).astype(BF16)

        gate = jnp.dot(km_ref[0, h], qt, preferred_element_type=F32,
                       precision=lax.Precision.HIGHEST)
        gate = jnp.where(valid, gate, NEG_INF)
        cnt = jnp.zeros((nblk, tq), F32)
        for jp in range(nblk):
            gj = gate[jp:jp + 1, :]
            beats = (gj > gate) | ((gj == gate) & (blk > jp))
            cnt = cnt + jnp.where(beats, 1.0, 0.0)
        sel_ref[h] = jnp.where(valid & (cnt < MOBA_TOPK), 1.0, 0.0)

        s = jnp.dot(kr_ref[0, h, i], qs_ref[h], preferred_element_type=F32)
        s = jnp.where(krow <= qcol, s, NEG_INF)
        m0 = jnp.max(s, axis=0, keepdims=True)
        p = jnp.exp2(s - m0)
        acc_ref[h] = jnp.dot(vt_ref[0, h, i], p.astype(BF16), preferred_element_type=F32)
        stats += [m0, jnp.sum(p, axis=0, keepdims=True)]

    def put_scores(slot, j):
        for h in range(nh):
            s = jnp.dot(kr_ref[0, h, j], qs_ref[h], preferred_element_type=F32)
            s_ref[slot, h] = jnp.where(sel_ref[h, pl.ds(j, 1), :] > 0.5, s, NEG_INF)

    def consume(slot, j, stats):
        out = []
        for h in range(nh):
            m, l = stats[2 * h], stats[2 * h + 1]
            m_new = jnp.maximum(m, jnp.max(s_ref[slot, h], axis=0, keepdims=True))
            alpha = jnp.exp2(m - m_new)
            l = alpha * l
            for r in range(0, tq, SOFTMAX_ROWS):
                p = jnp.exp2(s_ref[slot, h, r:r + SOFTMAX_ROWS, :] - m_new)
                l = l + jnp.sum(p, axis=0, keepdims=True)
                p_ref[h, r:r + SOFTMAX_ROWS, :] = p.astype(BF16)
            acc_ref[h] = alpha * acc_ref[h] + jnp.dot(vt_ref[0, h, j], p_ref[h],
                                                      preferred_element_type=F32)
            out += [m_new, l]
        return out

    put_scores(0, 0)

    def body(t, stats):
        first = 2 * t
        second = jnp.minimum(first + 1, nblk - 1)
        put_scores(1, second)
        stats = consume(0, first, stats)
        put_scores(0, jnp.minimum(first + 2, nblk - 1))
        return tuple(consume(1, second, stats))

    stats = lax.fori_loop(0, (i + 1) // 2, body, tuple(stats))
    for h in range(nh):
        o_ref[0, :, h * hd:(h + 1) * hd] = (acc_ref[h] / stats[2 * h + 1]).T.astype(BF16)


def _moba(qt, kr, vt, km):
    bsz, nh, nblk, hd, tq = qt.shape
    seq = nblk * tq
    full = lambda shape: pl.BlockSpec((1, nh) + shape, lambda b, i: (b, 0) + (0,) * len(shape))
    return pl.pallas_call(
        functools.partial(_moba_kernel, nblk=nblk),
        out_shape=jax.ShapeDtypeStruct((bsz, seq, nh * hd), BF16),
        grid=(bsz, nblk),
        in_specs=[
            pl.BlockSpec((1, nh, 1, hd, tq), lambda b, i: (b, 0, i, 0, 0)),
            full((nblk, tq, hd)), full((nblk, hd, tq)), full((nblk, hd)),
        ],
        out_specs=pl.BlockSpec((1, tq, nh * hd), lambda b, i: (b, i, 0)),
        scratch_shapes=[pltpu.VMEM((nh, nblk, tq), F32), pltpu.VMEM((nh, hd, tq), BF16),
                        pltpu.VMEM((nh, hd, tq), F32), pltpu.VMEM((nh, tq, tq), BF16),
                        pltpu.VMEM((2, nh, tq, tq), F32)],
        compiler_params=_params(("parallel", "arbitrary")),
        name="moba",
    )(qt, kr, vt, km)


def _softplus(x):
    return jnp.maximum(x, 0.0) + jnp.log1p(jnp.exp(-jnp.abs(x)))


def _ssd_kernel(xbc_ref, z_ref, dt_ref, cw_ref, cb_ref, dtb_ref, alog_ref, dsk_ref, nw_ref,
                o_ref, prev_ref, state_ref, y_ref):
    L, P, N = SSM_CHUNK, SSM_HEAD_DIM, SSM_STATE
    hi = lax.Precision.HIGHEST

    @pl.when(pl.program_id(1) == 0)
    def _():
        prev_ref[...] = jnp.zeros_like(prev_ref)
        state_ref[...] = jnp.zeros_like(state_ref)

    cur = xbc_ref[0]
    prev = prev_ref[...]
    crow = lax.broadcasted_iota(jnp.int32, cur.shape, 0)
    acc = cur * cw_ref[SSM_CONV - 1:SSM_CONV, :] + cb_ref[...]
    for j in range(1, SSM_CONV):
        shifted = pltpu.roll(jnp.where(crow >= L - j, prev, cur), j, 0)
        acc = acc + shifted * cw_ref[SSM_CONV - 1 - j:SSM_CONV - j, :]
    prev_ref[...] = cur
    xc = _silu(acc)
    xs = xc[:, :SSM_WIDTH]
    bm = xc[:, SSM_WIDTH:SSM_WIDTH + SSM_GROUPS * N].astype(BF16)
    cm = xc[:, SSM_WIDTH + SSM_GROUPS * N:].astype(BF16)
    xs_b = xs.astype(BF16)
    xs_t = xs.T

    dt = _softplus(dt_ref[0] + dtb_ref[...])
    da = dt * (-jnp.exp(alog_ref[...]))
    row = lax.broadcasted_iota(jnp.int32, (L, L), 0)
    col = lax.broadcasted_iota(jnp.int32, (L, L), 1)
    causal = row >= col
    acum = jnp.dot(jnp.where(causal, 1.0, 0.0), da, preferred_element_type=F32, precision=hi)
    acum_t = jnp.dot(da.T, jnp.where(row <= col, 1.0, 0.0), preferred_element_type=F32, precision=hi)
    dt_t = dt.T
    last_t = acum_t[:, L - 1:L]
    w_end_t = jnp.exp(last_t - acum_t) * dt_t
    e_last_t = jnp.exp(last_t)
    e_acum = jnp.exp(acum)

    for g in range(SSM_GROUPS):
        b_g = bm[:, g * N:(g + 1) * N]
        c_g = cm[:, g * N:(g + 1) * N]
        cb = lax.dot_general(c_g, b_g, (((1,), (1,)), ((), ())), preferred_element_type=F32)
        for hh in range(SSM_HEADS // SSM_GROUPS):
            h = g * (SSM_HEADS // SSM_GROUPS) + hh
            hs = slice(h * P, (h + 1) * P)
            decay = jnp.exp(jnp.where(causal, acum[:, h:h + 1] - acum_t[h:h + 1, :], -jnp.inf))
            mat = (cb * decay * dt_t[h:h + 1, :]).astype(BF16)
            y = jnp.dot(mat, xs_b[:, hs], preferred_element_type=F32)
            st = state_ref[hs, :]
            y = y + lax.dot_general(c_g, st.astype(BF16), (((1,), (1,)), ((), ())),
                                    preferred_element_type=F32) * e_acum[:, h:h + 1]
            y_ref[:, hs] = y
            xw = (xs_t[hs, :] * w_end_t[h:h + 1, :]).astype(BF16)
            state_ref[hs, :] = st * e_last_t[h:h + 1, :] + jnp.dot(xw, b_g, preferred_element_type=F32)

    y = y_ref[...] + dsk_ref[...] * xs
    y = y * _silu(z_ref[0])
    gw = SSM_WIDTH // SSM_GROUPS
    for g in range(SSM_GROUPS):
        cols = slice(g * gw, (g + 1) * gw)
        yg = y[:, cols]
        yg = yg * lax.rsqrt(jnp.mean(yg * yg, axis=-1, keepdims=True) + NORM_EPS)
        o_ref[0, :, cols] = (yg * nw_ref[:, cols]).astype(BF16)


def _ssd(proj, dt_raw, conv_w, conv_b, dt_bias, a_log, d_skip, ssm_norm):
    bsz, seq, _ = proj.shape
    L = SSM_CHUNK
    vec = lambda n: pl.BlockSpec((1, n), lambda b, c: (0, 0))
    return pl.pallas_call(
        _ssd_kernel,
        out_shape=jax.ShapeDtypeStruct((bsz, seq, SSM_WIDTH), BF16),
        grid=(bsz, seq // L),
        in_specs=[
            pl.BlockSpec((1, L, SSM_CONV_DIM), lambda b, c: (b, c, 0)),
            pl.BlockSpec((1, L, SSM_WIDTH), lambda b, c: (b, c, OFF_Z // SSM_WIDTH)),
            pl.BlockSpec((1, L, LANES), lambda b, c: (b, c, 0)),
            pl.BlockSpec((SSM_CONV, SSM_CONV_DIM), lambda b, c: (0, 0)),
            vec(SSM_CONV_DIM), vec(LANES), vec(LANES), vec(SSM_WIDTH), vec(SSM_WIDTH),
        ],
        out_specs=pl.BlockSpec((1, L, SSM_WIDTH), lambda b, c: (b, c, 0)),
        scratch_shapes=[
            pltpu.VMEM((L, SSM_CONV_DIM), F32),
            pltpu.VMEM((SSM_WIDTH, SSM_STATE), F32),
            pltpu.VMEM((L, SSM_WIDTH), F32),
        ],
        compiler_params=_params(("parallel", "arbitrary")),
        name="ssd",
    )(proj, proj, dt_raw, conv_w, conv_b, dt_bias, a_log, d_skip, ssm_norm)


def _route(logits):
    lane = lax.broadcasted_iota(jnp.int32, logits.shape, 1).astype(F32)
    big = float(LANES)

    def first_argmax(vals, mask):
        top = jnp.max(vals, axis=-1, keepdims=True)
        idx = jnp.min(jnp.where(mask & (vals >= top), lane, big), axis=-1, keepdims=True)
        return top, idx

    def masked_softmax(mask):
        z = jnp.where(mask, logits, -jnp.inf)
        e = jnp.exp(z - jnp.max(z, axis=-1, keepdims=True))
        return e / jnp.sum(e, axis=-1, keepdims=True)

    gmask = lane < MOE_GROUPS
    g_prob, g_idx = first_argmax(jnp.where(gmask, masked_softmax(gmask), -1.0), gmask)
    lo = MOE_GROUPS + g_idx * MOE_EPG
    fmask = (lane >= lo) & (lane < lo + MOE_EPG)
    p_exp = jnp.where(fmask, masked_softmax(fmask), -1.0)
    p1, i1 = first_argmax(p_exp, fmask)
    rest = fmask & (lane != i1)
    p2, i2 = first_argmax(jnp.where(rest, p_exp, -1.0), rest)
    den = p1 + p2
    out = jnp.where(lane == 0, i1 - MOE_GROUPS, 0.0)
    out = jnp.where(lane == 1, i2 - MOE_GROUPS, out)
    out = jnp.where(lane == 2, g_prob * p1 / den, out)
    out = jnp.where(lane == 3, g_prob * p2 / den, out)
    return out


def _outproj_kernel(ya_ref, yb_ref, yc_ref, x_ref, g1_ref, wa_ref, wb_ref, wc_ref,
                    n2_ref, sc_ref, sh_ref, wr_ref, br_ref, xo_ref, hc_ref, rt_ref):
    mix = jnp.dot(ya_ref[0], wa_ref[...], preferred_element_type=F32)
    mix = mix + jnp.dot(yb_ref[0], wb_ref[...], preferred_element_type=F32)
    mix = mix + jnp.dot(yc_ref[0], wc_ref[...], preferred_element_type=F32)
    xn = x_ref[0] + g1_ref[0] * mix
    xo_ref[0] = xn
    h = _modulated_norm(xn, n2_ref[...], sc_ref[0], sh_ref[0])
    half = N_PACK * LANES
    packed = _pack_halves(h[:, :half], h[:, half:])
    for c in range(N_PACK):
        hc_ref[0, c] = packed[:, c * LANES:(c + 1) * LANES]
    h_hi = h.astype(BF16)
    h_lo = (h - h_hi.astype(F32)).astype(BF16)
    part = jnp.dot(h_hi, wr_ref[...], preferred_element_type=F32)
    logits = (part[:, :LANES] + part[:, LANES:]
              + jnp.dot(h_lo, wr_ref[:, :LANES], preferred_element_type=F32) + br_ref[...])
    rt_ref[0] = _route(logits)


TM_TOK = 512


def _out_proj_route(ya, yb, yc, x, g1, wa, wb, wc, n2, sc2, sh2, w_r, b_r):
    bsz, seq, d = x.shape
    tm = TM_TOK
    nt = seq // tm
    row = lambda w: pl.BlockSpec((1, tm, w), lambda b, i: (b, i, 0))
    per_b = pl.BlockSpec((1, 1, d), lambda b, i: (b, 0, 0))
    const = lambda a: pl.BlockSpec(a.shape, lambda b, i: (0,) * a.ndim,
                                   pipeline_mode=pl.Buffered(1))
    return pl.pallas_call(
        _outproj_kernel,
        out_shape=(jax.ShapeDtypeStruct((bsz, seq, d), F32),
                   jax.ShapeDtypeStruct((bsz * nt, N_PACK, tm, LANES), U32),
                   jax.ShapeDtypeStruct((bsz, seq, LANES), F32)),
        grid=(bsz, nt),
        in_specs=[row(ya.shape[-1]), row(yb.shape[-1]), row(yc.shape[-1]), row(d), per_b,
                  const(wa), const(wb), const(wc), const(n2), per_b, per_b, const(w_r), const(b_r)],
        out_specs=(row(d),
                   pl.BlockSpec((1, N_PACK, tm, LANES), lambda b, i: (b * nt + i, 0, 0, 0)),
                   row(LANES)),
        compiler_params=_params(("parallel", "parallel")),
        name="out_proj_route",
    )(ya, yb, yc, x, g1, wa, wb, wc, n2, sc2, sh2, w_r, b_r)


BLOCK_SHIFT = MOE_BLOCK.bit_length() - 1


TOK_SHIFT = TM_TOK.bit_length() - 1
MXU_WIDTH = 256


def _expert_kernel(be_ref, nv_ref, tok_ref, h_hbm, wg_ref, wu_ref, wd_ref, o_ref,
                   buf_ref, x_ref, wgb_ref, wub_ref, wdb_ref, sem):
    i = pl.program_id(0)
    slot = i % 2
    half = N_PACK * LANES

    def gather(blk, slot_, mi):
        tok = tok_ref[blk * MOE_BLOCK + mi]
        return pltpu.make_async_copy(
            h_hbm.at[jnp.right_shift(tok, TOK_SHIFT), :, jnp.bitwise_and(tok, TM_TOK - 1), :],
            buf_ref.at[slot_, :, mi, :], sem.at[slot_])

    def for_valid_rows(blk, fn):
        def body(mi, carry):
            fn(mi)
            return carry
        lax.fori_loop(0, nv_ref[blk], body, 0)

    @pl.when(i == 0)
    def _():
        buf_ref[...] = jnp.zeros_like(buf_ref)
        for_valid_rows(0, lambda mi: gather(0, 0, mi).start())

    @pl.when(i + 1 < pl.num_programs(0))
    def _():
        for_valid_rows(i + 1, lambda mi: gather(i + 1, 1 - slot, mi).start())

    @pl.when((i == 0) | (be_ref[i] != be_ref[jnp.maximum(i - 1, 0)]))
    def _():
        wgb_ref[...] = wg_ref[0, 0].astype(BF16)
        wub_ref[...] = wu_ref[0, 0].astype(BF16)
        wdb_ref[...] = wd_ref[0, 0].astype(BF16)

    for_valid_rows(i, lambda mi: gather(i, slot, mi).wait())

    @pl.when(nv_ref[i] == 0)
    def _():
        o_ref[...] = jnp.zeros_like(o_ref)

    @pl.when(nv_ref[i] > 0)
    def _():
        for c in range(N_PACK):
            lo, hi = _unpack_halves(buf_ref[slot, c])
            x_ref[:, c * LANES:(c + 1) * LANES] = lo.astype(BF16)
            x_ref[:, half + c * LANES:half + (c + 1) * LANES] = hi.astype(BF16)
        ff = wgb_ref.shape[1]
        hid = []
        for p in range(ff // MXU_WIDTH):
            cols = slice(p * MXU_WIDTH, (p + 1) * MXU_WIDTH)
            gate = jnp.dot(x_ref[...], wgb_ref[:, cols], preferred_element_type=F32)
            up = jnp.dot(x_ref[...], wub_ref[:, cols], preferred_element_type=F32)
            hid.append((_silu(gate) * up).astype(BF16))
        hid = jnp.concatenate(hid, axis=-1)
        for p in range(half // MXU_WIDTH):
            cols = slice(p * MXU_WIDTH, (p + 1) * MXU_WIDTH)
            y_lo = jnp.dot(hid, wdb_ref[:, cols], preferred_element_type=F32)
            y_hi = jnp.dot(hid, wdb_ref[:, half + p * MXU_WIDTH:half + (p + 1) * MXU_WIDTH],
                           preferred_element_type=F32)
            packed = _pack_halves(y_lo, y_hi)
            for c in range(MXU_WIDTH // LANES):
                o_ref[0, p * (MXU_WIDTH // LANES) + c] = packed[:, c * LANES:(c + 1) * LANES]


def _experts(layer, blk_exp, blk_rows, row_tok, h_chunks, w_gate, w_up, w_down):
    n_blocks = blk_exp.shape[0]
    d, ff = w_gate.shape[2], w_gate.shape[3]
    return pl.pallas_call(
        _expert_kernel,
        out_shape=jax.ShapeDtypeStruct((n_blocks, N_PACK, MOE_BLOCK, LANES), U32),
        grid_spec=pltpu.PrefetchScalarGridSpec(
            num_scalar_prefetch=3,
            grid=(n_blocks,),
            in_specs=[pl.BlockSpec(memory_space=pl.ANY)] + [
                pl.BlockSpec(shape, lambda i, be, nv, rt: (layer, be[i], 0, 0))
                for shape in ((1, 1, d, ff), (1, 1, d, ff), (1, 1, ff, d))],
            out_specs=pl.BlockSpec((1, N_PACK, MOE_BLOCK, LANES), lambda i, be, nv, rt: (i, 0, 0, 0)),
            scratch_shapes=[pltpu.VMEM((2, N_PACK, MOE_BLOCK, LANES), U32),
                            pltpu.VMEM((MOE_BLOCK, d), BF16),
                            pltpu.VMEM((d, ff), BF16), pltpu.VMEM((d, ff), BF16),
                            pltpu.VMEM((ff, d), BF16), pltpu.SemaphoreType.DMA((2,))],
        ),
        compiler_params=_params(("arbitrary",)),
        name="moe_experts",
    )(blk_exp, blk_rows, row_tok, h_chunks, w_gate, w_up, w_down)


def _combine_kernel(dest_ref, y_hbm, x_ref, g2_ref, rt_ref, fn_ref, o_ref, buf_ref, sem, *, final):
    i = pl.program_id(0)
    tm = TM_TOK

    def gather(k, mi):
        d = dest_ref[(i * tm + mi) * 2 + k]
        return pltpu.make_async_copy(
            y_hbm.at[jnp.right_shift(d, BLOCK_SHIFT), :, jnp.bitwise_and(d, MOE_BLOCK - 1), :],
            buf_ref.at[k, :, mi, :], sem)

    def issue(mi, carry):
        for k in range(2):
            gather(k, mi).start()
        return carry

    def drain(mi, carry):
        for k in range(2):
            gather(k, mi).wait()
        return carry

    lax.fori_loop(0, tm, issue, 0)
    lax.fori_loop(0, tm, drain, 0)
    rt = rt_ref[...]
    gate0, gate1 = rt[:, 2:3], rt[:, 3:4]
    for c in range(N_PACK):
        halves = zip(_unpack_halves(buf_ref[0, c]), _unpack_halves(buf_ref[1, c]))
        for part, (y0, y1) in enumerate(halves):
            cols = slice((part * N_PACK + c) * LANES, (part * N_PACK + c + 1) * LANES)
            o_ref[:, cols] = x_ref[:, cols] + g2_ref[0, :, cols] * (y0 * gate0 + y1 * gate1)
    if final:
        x = o_ref[...]
        ms = jnp.mean(x * x, axis=-1, keepdims=True)
        o_ref[...] = (x * lax.rsqrt(ms + NORM_EPS)) * fn_ref[...]


def _combine(dest_flat, y_rows, x2d, g2, route2d, seq, final_g, final):
    n_tok, d = x2d.shape
    tm = TM_TOK
    per_seq = seq // tm
    return pl.pallas_call(
        functools.partial(_combine_kernel, final=final),
        out_shape=jax.ShapeDtypeStruct((n_tok, d), F32),
        grid_spec=pltpu.PrefetchScalarGridSpec(
            num_scalar_prefetch=1,
            grid=(n_tok // tm,),
            in_specs=[pl.BlockSpec(memory_space=pl.ANY),
                      pl.BlockSpec((tm, d), lambda i, dr: (i, 0)),
                      pl.BlockSpec((1, 1, d), lambda i, dr: (i // per_seq, 0, 0)),
                      pl.BlockSpec((tm, LANES), lambda i, dr: (i, 0)),
                      pl.BlockSpec((1, d), lambda i, dr: (0, 0))],
            out_specs=pl.BlockSpec((tm, d), lambda i, dr: (i, 0)),
            scratch_shapes=[pltpu.VMEM((2, N_PACK, tm, LANES), U32), pltpu.SemaphoreType.DMA],
        ),
        compiler_params=_params(("arbitrary",)),
        name="moe_combine",
    )(dest_flat, y_rows, x2d, g2, route2d, final_g)


def _dispatch_plan(route2d):
    n_tok = route2d.shape[0]
    n_asg = n_tok * 2
    flat_e = route2d[:, 0:2].astype(jnp.int32).reshape(n_asg)
    asg = jnp.arange(n_asg, dtype=jnp.int32)
    s_e, order = lax.sort((flat_e, asg), num_keys=1)
    ids = jnp.arange(MOE_EXPERTS, dtype=jnp.int32)
    start = jnp.sum(s_e[:, None] < ids[None, :], axis=0, dtype=jnp.int32)
    counts = jnp.sum(s_e[:, None] == ids[None, :], axis=0, dtype=jnp.int32)
    pcounts = (counts + MOE_BLOCK - 1) // MOE_BLOCK * MOE_BLOCK
    pend = jnp.cumsum(pcounts)
    pstart = pend - pcounts
    dest_sorted = pstart[s_e] + asg - start[s_e]
    _, dest = lax.sort((order, dest_sorted), num_keys=1)
    n_blocks = n_asg // MOE_BLOCK + MOE_EXPERTS
    blk_start = jnp.arange(n_blocks, dtype=jnp.int32) * MOE_BLOCK
    blk_exp = jnp.minimum(jnp.sum(pend[None, :] <= blk_start[:, None], axis=1, dtype=jnp.int32),
                          MOE_EXPERTS - 1)
    row = jnp.arange(n_blocks * MOE_BLOCK, dtype=jnp.int32)
    row_e = jnp.repeat(blk_exp, MOE_BLOCK)
    off = row - pstart[row_e]
    src = jnp.minimum(start[row_e] + off, n_asg - 1)
    row_tok = jnp.where(off < counts[row_e], order[src] // 2, 0).astype(jnp.int32)
    blk_rows = jnp.clip(counts[blk_exp] - (blk_start - pstart[blk_exp]), 0, MOE_BLOCK).astype(jnp.int32)
    return dest, row_tok, blk_exp, blk_rows


def _rotary_tables(seq):
    pos = jnp.arange(seq, dtype=F32)
    inv_freq = ROPE_THETA ** (-jnp.arange(0, ROPE_DIM, 2, dtype=F32) / ROPE_DIM)
    ang = pos[:, None] * inv_freq[None, :]
    cos, sin = jnp.cos(ang), jnp.sin(ang)
    half = ROPE_DIM // 2
    ones = jnp.ones((seq, LANES - ROPE_DIM), F32)
    tab_c = jnp.concatenate([cos, cos, ones], axis=-1)
    tab_s1 = jnp.concatenate([-sin, jnp.zeros((seq, LANES - half), F32)], axis=-1)
    tab_s2 = jnp.concatenate([jnp.zeros((seq, half), F32), sin,
                              jnp.zeros((seq, LANES - ROPE_DIM), F32)], axis=-1)
    return tab_c, tab_s1, tab_s2


def _pad_lanes(v):
    return jnp.pad(v, (0, LANES - v.shape[0])).reshape(1, LANES)


def kernel(x, c, norm1, norm2, w_ada, b_ada, w_in, w_out, sgu_ln_g, sgu_ln_b, sgu_w, sgu_b, conv_w, conv_b, dt_bias, a_log, d_skip, ssm_norm, w_coarse, b_coarse, w_fine, b_fine, w_gate, w_up, w_down, final_norm):
    bsz, seq, d = x.shape
    depth = w_in.shape[0]
    n_tok = bsz * seq

    c_pad = jnp.pad(c, ((0, 8 - bsz % 8 if bsz % 8 else 0), (0, 0)))
    mod = _ada_mod(c_pad, w_ada, b_ada)[:, :bsz].reshape(depth, bsz, 6, 1, d)
    tab_c, tab_s1, tab_s2 = _rotary_tables(seq)

    for l in range(depth):
        sh1, sc1, g1, sh2, sc2, g2 = [mod[l, :, i] for i in range(6)]
        wl = w_in[l]
        w_main = jnp.concatenate(
            [wl[:, 3584:5120], wl[:, 0:512], wl[:, 2560:3584], wl[:, 512:1024], wl[:, 1024:2560]],
            axis=1).astype(BF16)
        w_dt = jnp.pad(wl[:, 5120:], ((0, 0), (0, LANES - SSM_HEADS))).astype(BF16)
        proj, dt_raw = _in_proj(x, norm1[l].reshape(1, d), sc1, sh1, w_main, w_dt)

        ya = _sgu(proj, sgu_ln_g[l].reshape(1, -1), sgu_ln_b[l].reshape(1, -1), sgu_w[l], sgu_b[l].T)
        qt, kr, vt, km = _rope_prep(proj, tab_c, tab_s1, tab_s2)
        yb = _moba(qt, kr, vt, km.reshape(km.shape[:3] + (ATT_HEAD_DIM,)))
        yc = _ssd(proj, dt_raw, conv_w[l], conv_b[l].reshape(1, -1), _pad_lanes(dt_bias[l]),
                  _pad_lanes(a_log[l]), jnp.repeat(d_skip[l], SSM_HEAD_DIM).reshape(1, -1),
                  ssm_norm[l].reshape(1, -1))

        wo = w_out[l].astype(BF16)
        w_r = jnp.concatenate([w_coarse[l], w_fine[l].transpose(1, 0, 2).reshape(d, MOE_EXPERTS)], axis=1)
        w_r = jnp.pad(w_r, ((0, 0), (0, LANES - w_r.shape[1])))
        w_r_hi = w_r.astype(BF16)
        w_r = jnp.concatenate([w_r_hi, (w_r - w_r_hi.astype(F32)).astype(BF16)], axis=1)
        b_r = _pad_lanes(jnp.concatenate([b_coarse[l], b_fine[l].reshape(-1)]))
        x, h_chunks, route = _out_proj_route(
            ya, yb, yc, x, g1, wo[:SGU_WIDTH], wo[SGU_WIDTH:2 * SGU_WIDTH], wo[2 * SGU_WIDTH:],
            norm2[l].reshape(1, d), sc2, sh2, w_r, b_r)

        route2d = route.reshape(n_tok, LANES)
        dest, row_tok, blk_exp, blk_rows = _dispatch_plan(route2d)
        y_rows = _experts(l, blk_exp, blk_rows, row_tok, h_chunks, w_gate, w_up, w_down)
        x = _combine(dest, y_rows, x.reshape(n_tok, d), g2, route2d, seq,
                     final_norm.reshape(1, d), final=(l == depth - 1)).reshape(bsz, seq, d)

    return x
```

```python
import functools

import jax
import jax.numpy as jnp
from jax import lax
from jax.experimental import pallas as pl
from jax.experimental.pallas import tpu as pltpu

F32 = jnp.float32
BF16 = jnp.bfloat16

LANES = 128
NORM_EPS = 1e-6
NEG_INF = -1e30

D_MODEL = 2048
N_CHUNK = D_MODEL // LANES
N_PACK = N_CHUNK // 2
SGU_WIDTH = 512
SGU_GROUPS = 4
SGU_CHUNK = 128
ATT_HEADS = 4
ATT_HEAD_DIM = 128
ROPE_DIM = 32
ROPE_THETA = 500000.0
MOBA_BLOCK = 256
MOBA_TOPK = 3
SSM_WIDTH = 1024
SSM_HEAD_DIM = 64
SSM_HEADS = 16
SSM_GROUPS = 2
SSM_STATE = 128
SSM_CONV = 4
SSM_CHUNK = 128
SSM_CONV_DIM = 1536
MOE_GROUPS = 4
MOE_EPG = 8
MOE_EXPERTS = 32
MOE_FF = 512
MOE_BLOCK = 512

PROJ_MAIN = 5120
OFF_U, OFF_Z, OFF_V, OFF_Q, OFF_K, OFF_VA = 1536, 2048, 3072, 3584, 4096, 4608

VMEM_LIMIT = 56 * 1024 * 1024


def _params(sem, vmem=VMEM_LIMIT):
    return pltpu.CompilerParams(dimension_semantics=sem, vmem_limit_bytes=vmem)


def _silu(x):
    return x * jax.nn.sigmoid(x)


U32 = jnp.uint32
HI_MASK = 0xFFFF0000


def _pack_halves(lo, hi):
    lo_bits = lax.bitcast_convert_type(lo.astype(BF16).astype(F32), U32)
    hi_bits = lax.bitcast_convert_type(hi.astype(BF16).astype(F32), U32)
    return jnp.bitwise_or(jnp.bitwise_and(hi_bits, U32(HI_MASK)), jnp.right_shift(lo_bits, U32(16)))


def _unpack_halves(w):
    lo = lax.bitcast_convert_type(jnp.left_shift(w, U32(16)), F32)
    hi = lax.bitcast_convert_type(jnp.bitwise_and(w, U32(HI_MASK)), F32)
    return lo, hi


def _ada_kernel(c_ref, w_ref, b_ref, o_ref):
    ca = _silu(c_ref[...]).astype(BF16)
    o_ref[0] = jnp.dot(ca, w_ref[0].astype(BF16), preferred_element_type=F32) + b_ref[0]


def _ada_mod(c_pad, w_ada, b_ada):
    depth, d, n = w_ada.shape
    tn = 1024
    return pl.pallas_call(
        _ada_kernel,
        out_shape=jax.ShapeDtypeStruct((depth, c_pad.shape[0], n), F32),
        grid=(depth, n // tn),
        in_specs=[
            pl.BlockSpec(c_pad.shape, lambda l, j: (0, 0)),
            pl.BlockSpec((1, d, tn), lambda l, j: (l, 0, j)),
            pl.BlockSpec((1, 1, tn), lambda l, j: (l, 0, j)),
        ],
        out_specs=pl.BlockSpec((1, c_pad.shape[0], tn), lambda l, j: (l, 0, j)),
        compiler_params=_params(("parallel", "parallel")),
        name="ada_mod",
    )(c_pad, w_ada, b_ada.reshape(depth, 1, n))


def _modulated_norm(x, g, sc, sh):
    ms = jnp.mean(x * x, axis=-1, keepdims=True)
    return (x * lax.rsqrt(ms + NORM_EPS)) * g * (1.0 + sc) + sh


def _inproj_kernel(x_ref, g_ref, sc_ref, sh_ref, w_ref, wdt_ref, o_ref, dt_ref, h_ref):
    @pl.when(pl.program_id(2) == 0)
    def _():
        h = _modulated_norm(x_ref[0], g_ref[...], sc_ref[0], sh_ref[0]).astype(BF16)
        h_ref[...] = h
        dt_ref[0] = jnp.dot(h, wdt_ref[...], preferred_element_type=F32)

    o_ref[0] = jnp.dot(h_ref[...], w_ref[...], preferred_element_type=F32)


def _in_proj(x, g, sc, sh, w, wdt):
    bsz, seq, d = x.shape
    n = w.shape[1]
    tm, tn = 512, 1024
    return pl.pallas_call(
        _inproj_kernel,
        out_shape=(jax.ShapeDtypeStruct((bsz, seq, n), F32),
                   jax.ShapeDtypeStruct((bsz, seq, LANES), F32)),
        grid=(bsz, seq // tm, n // tn),
        in_specs=[
            pl.BlockSpec((1, tm, d), lambda b, i, j: (b, i, 0)),
            pl.BlockSpec((1, d), lambda b, i, j: (0, 0)),
            pl.BlockSpec((1, 1, d), lambda b, i, j: (b, 0, 0)),
            pl.BlockSpec((1, 1, d), lambda b, i, j: (b, 0, 0)),
            pl.BlockSpec((d, tn), lambda b, i, j: (0, j)),
            pl.BlockSpec((d, LANES), lambda b, i, j: (0, 0)),
        ],
        out_specs=(pl.BlockSpec((1, tm, tn), lambda b, i, j: (b, i, j)),
                   pl.BlockSpec((1, tm, LANES), lambda b, i, j: (b, i, 0))),
        scratch_shapes=[pltpu.VMEM((tm, d), BF16)],
        compiler_params=_params(("parallel", "parallel", "arbitrary")),
        name="in_proj",
    )(x, g, sc, sh, w, wdt)


def _sgu_kernel(u_ref, v_ref, g_ref, b_ref, w_ref, bs_ref, o_ref, *, n_chunks):
    row = lax.broadcasted_iota(jnp.int32, (SGU_CHUNK, SGU_CHUNK), 0)
    col = lax.broadcasted_iota(jnp.int32, (SGU_CHUNK, SGU_CHUNK), 1)
    w_causal = [jnp.where(row >= col, w_ref[g], 0.0).astype(BF16) for g in range(SGU_GROUPS)]
    for ci in range(n_chunks):
        rows = slice(ci * SGU_CHUNK, (ci + 1) * SGU_CHUNK)
        u = jax.nn.gelu(u_ref[0, rows, :])
        vf = jax.nn.gelu(v_ref[0, rows, :])
        mu = jnp.mean(vf, axis=-1, keepdims=True)
        var = jnp.mean(jnp.square(vf - mu), axis=-1, keepdims=True)
        vn = ((vf - mu) * lax.rsqrt(var + NORM_EPS) * g_ref[...] + b_ref[...]).astype(BF16)
        for g in range(SGU_GROUPS):
            cols = slice(g * LANES, (g + 1) * LANES)
            s = jnp.dot(w_causal[g], vn[:, cols], preferred_element_type=F32) + bs_ref[:, g:g + 1]
            o_ref[0, rows, cols] = (u[:, cols] * s).astype(BF16)


def _sgu(proj, ln_g, ln_b, w_s, b_s_t):
    bsz, seq, _ = proj.shape
    tc = 512
    blk = lambda off: pl.BlockSpec((1, tc, SGU_WIDTH), lambda b, i: (b, i, off // SGU_WIDTH))
    return pl.pallas_call(
        functools.partial(_sgu_kernel, n_chunks=tc // SGU_CHUNK),
        out_shape=jax.ShapeDtypeStruct((bsz, seq, SGU_WIDTH), BF16),
        grid=(bsz, seq // tc),
        in_specs=[
            blk(OFF_U), blk(OFF_V),
            pl.BlockSpec((1, SGU_WIDTH), lambda b, i: (0, 0)),
            pl.BlockSpec((1, SGU_WIDTH), lambda b, i: (0, 0)),
            pl.BlockSpec((SGU_GROUPS, SGU_CHUNK, SGU_CHUNK), lambda b, i: (0, 0, 0)),
            pl.BlockSpec((SGU_CHUNK, SGU_GROUPS), lambda b, i: (0, 0)),
        ],
        out_specs=pl.BlockSpec((1, tc, SGU_WIDTH), lambda b, i: (b, i, 0)),
        compiler_params=_params(("parallel", "parallel")),
        name="sgu",
    )(proj, proj, ln_g, ln_b, w_s, b_s_t)


def _rotary(t, c, s1, s2):
    half = ROPE_DIM // 2
    return t * c + pltpu.roll(t, LANES - half, 1) * s1 + pltpu.roll(t, half, 1) * s2


def _rope_kernel(q_ref, k_ref, v_ref, c_ref, s1_ref, s2_ref, qt_ref, kr_ref, vt_ref, km_ref):
    c, s1, s2 = c_ref[...], s1_ref[...], s2_ref[...]
    for h in range(ATT_HEADS):
        cols = slice(h * ATT_HEAD_DIM, (h + 1) * ATT_HEAD_DIM)
        q = _rotary(q_ref[0, :, cols], c, s1, s2)
        k = _rotary(k_ref[0, :, cols], c, s1, s2)
        qt_ref[0, h, 0] = q.T
        kr_ref[0, h, 0] = k.astype(BF16)
        vt_ref[0, h, 0] = v_ref[0, :, cols].T.astype(BF16)
        km_ref[0, h, 0] = jnp.mean(k, axis=0, keepdims=True)


def _rope_prep(proj, tab_c, tab_s1, tab_s2):
    bsz, seq, _ = proj.shape
    nblk = seq // MOBA_BLOCK
    hd, nh = ATT_HEAD_DIM, ATT_HEADS
    width = nh * hd
    sec = lambda off: pl.BlockSpec((1, MOBA_BLOCK, width), lambda i, b: (b, i, off // width))
    tab = pl.BlockSpec((MOBA_BLOCK, hd), lambda i, b: (i, 0))
    out = lambda rows, cols: pl.BlockSpec((1, nh, 1, rows, cols), lambda i, b: (b, 0, i, 0, 0))
    return pl.pallas_call(
        _rope_kernel,
        out_shape=(
            jax.ShapeDtypeStruct((bsz, nh, nblk, hd, MOBA_BLOCK), F32),
            jax.ShapeDtypeStruct((bsz, nh, nblk, MOBA_BLOCK, hd), BF16),
            jax.ShapeDtypeStruct((bsz, nh, nblk, hd, MOBA_BLOCK), BF16),
            jax.ShapeDtypeStruct((bsz, nh, nblk, 1, hd), F32),
        ),
        grid=(nblk, bsz),
        in_specs=[sec(OFF_Q), sec(OFF_K), sec(OFF_VA), tab, tab, tab],
        out_specs=(out(hd, MOBA_BLOCK), out(MOBA_BLOCK, hd), out(hd, MOBA_BLOCK), out(1, hd)),
        compiler_params=_params(("parallel", "parallel")),
        name="rope_prep",
    )(proj, proj, proj, tab_c, tab_s1, tab_s2)


LOG2E = 1.4426950408889634
SOFTMAX_ROWS = 64


def _moba_kernel(qt_ref, kr_ref, vt_ref, km_ref, o_ref, sel_ref, qs_ref, acc_ref, p_ref, s_ref, *, nblk):
    i = pl.program_id(1)
    tq, nh, hd = MOBA_BLOCK, ATT_HEADS, ATT_HEAD_DIM
    blk = lax.broadcasted_iota(jnp.int32, (nblk, tq), 0)
    valid = blk < i
    blk_f = blk.astype(F32)
    krow =lax.broadcasted_iota(jnp.int32, (tq, tq), 0)
    qcol = lax.broadcasted_iota(jnp.int32, (tq, tq), 1)

    stats = []
    for h in range(nh):
        qt = qt_ref[0, h, 0]
        qs_ref[h] = (qt * (hd ** -0.5 * LOG2E)).astype(BF16)

        gate = jnp.dot(km_ref[0, h], qt, preferred_element_type=F32,
                       precision=lax.Precision.HIGHEST)
        gate = jnp.where(valid, gate, NEG_INF)
        picked = jnp.zeros((nblk, tq), jnp.bool_)
        for _ in range(MOBA_TOPK):
            top = jnp.max(gate, axis=0, keepdims=True)
            first = jnp.min(jnp.where(gate == top, blk_f, float(nblk)), axis=0, keepdims=True)
            pick = blk_f == first
            picked = picked | pick
            gate = jnp.where(pick, -jnp.inf, gate)
        sel_ref[h] = jnp.where(valid & picked, 1.0, 0.0)

        s = jnp.dot(kr_ref[0, h, i], qs_ref[h], preferred_element_type=F32)
        s = jnp.where(krow <= qcol, s, NEG_INF)
        m0 = jnp.max(s, axis=0, keepdims=True)
        p = jnp.exp2(s - m0)
        acc_ref[h] = jnp.dot(vt_ref[0, h, i], p.astype(BF16), preferred_element_type=F32)
        stats += [m0, jnp.sum(p, axis=0, keepdims=True)]

    def put_scores(slot, j):
        for h in range(nh):
            s = jnp.dot(kr_ref[0, h, j], qs_ref[h], preferred_element_type=F32)
            s_ref[slot, h] = jnp.where(sel_ref[h, pl.ds(j, 1), :] > 0.5, s, NEG_INF)

    def consume(slot, j, stats):
        out = []
        for h in range(nh):
            m, l = stats[2 * h], stats[2 * h + 1]
            m_new = jnp.maximum(m, jnp.max(s_ref[slot, h], axis=0, keepdims=True))
            alpha = jnp.exp2(m - m_new)
            l = alpha * l
            for r in range(0, tq, SOFTMAX_ROWS):
                p = jnp.exp2(s_ref[slot, h, r:r + SOFTMAX_ROWS, :] - m_new)
                l = l + jnp.sum(p, axis=0, keepdims=True)
                p_ref[h, r:r + SOFTMAX_ROWS, :] = p.astype(BF16)
            acc_ref[h] = alpha * acc_ref[h] + jnp.dot(vt_ref[0, h, j], p_ref[h],
                                                      preferred_element_type=F32)
            out += [m_new, l]
        return out

    put_scores(0, 0)

    def body(t, stats):
        first = 2 * t
        second = jnp.minimum(first + 1, nblk - 1)
        put_scores(1, second)
        stats = consume(0, first, stats)
        put_scores(0, jnp.minimum(first + 2, nblk - 1))
        return tuple(consume(1, second, stats))

    stats = lax.fori_loop(0, (i + 1) // 2, body, tuple(stats))
    for h in range(nh):
        o_ref[0, :, h * hd:(h + 1) * hd] = (acc_ref[h] / stats[2 * h + 1]).T.astype(BF16)


def _moba(qt, kr, vt, km):
    bsz, nh, nblk, hd, tq = qt.shape
    seq = nblk * tq
    full = lambda shape: pl.BlockSpec((1, nh) + shape, lambda b, i: (b, 0) + (0,) * len(shape))
    return pl.pallas_call(
        functools.partial(_moba_kernel, nblk=nblk),
        out_shape=jax.ShapeDtypeStruct((bsz, seq, nh * hd), BF16),
        grid=(bsz, nblk),
        in_specs=[
            pl.BlockSpec((1, nh, 1, hd, tq), lambda b, i: (b, 0, i, 0, 0)),
            full((nblk, tq, hd)), full((nblk, hd, tq)), full((nblk, hd)),
        ],
        out_specs=pl.BlockSpec((1, tq, nh * hd), lambda b, i: (b, i, 0)),
        scratch_shapes=[pltpu.VMEM((nh, nblk, tq), F32), pltpu.VMEM((nh, hd, tq), BF16),
                        pltpu.VMEM((nh, hd, tq), F32), pltpu.VMEM((nh, tq, tq), BF16),
                        pltpu.VMEM((2, nh, tq, tq), F32)],
        compiler_params=_params(("parallel", "arbitrary")),
        name="moba",
    )(qt, kr, vt, km)


def _softplus(x):
    return jnp.maximum(x, 0.0) + jnp.log1p(jnp.exp(-jnp.abs(x)))


def _ssd_kernel(xbc_ref, z_ref, dt_ref, cw_ref, cb_ref, dtb_ref, alog_ref, dsk_ref, nw_ref,
                o_ref, prev_ref, state_ref, y_ref):
    L, P, N = SSM_CHUNK, SSM_HEAD_DIM, SSM_STATE
    hi = lax.Precision.HIGHEST

    @pl.when(pl.program_id(1) == 0)
    def _():
        prev_ref[...] = jnp.zeros_like(prev_ref)
        state_ref[...] = jnp.zeros_like(state_ref)

    cur = xbc_ref[0]
    prev = prev_ref[...]
    crow = lax.broadcasted_iota(jnp.int32, cur.shape, 0)
    acc = cur * cw_ref[SSM_CONV - 1:SSM_CONV, :] + cb_ref[...]
    for j in range(1, SSM_CONV):
        shifted = pltpu.roll(jnp.where(crow >= L - j, prev, cur), j, 0)
        acc = acc + shifted * cw_ref[SSM_CONV - 1 - j:SSM_CONV - j, :]
    prev_ref[...] = cur
    xc = _silu(acc)
    xs = xc[:, :SSM_WIDTH]
    bm = xc[:, SSM_WIDTH:SSM_WIDTH + SSM_GROUPS * N].astype(BF16)
    cm = xc[:, SSM_WIDTH + SSM_GROUPS * N:].astype(BF16)
    xs_b = xs.astype(BF16)
    xs_t = xs.T

    dt = _softplus(dt_ref[0] + dtb_ref[...])
    da = dt * (-jnp.exp(alog_ref[...]))
    row = lax.broadcasted_iota(jnp.int32, (L, L), 0)
    col = lax.broadcasted_iota(jnp.int32, (L, L), 1)
    causal = row >= col
    acum = jnp.dot(jnp.where(causal, 1.0, 0.0), da, preferred_element_type=F32, precision=hi)
    acum_t = jnp.dot(da.T, jnp.where(row <= col, 1.0, 0.0), preferred_element_type=F32, precision=hi)
    dt_t = dt.T
    last_t = acum_t[:, L - 1:L]
    w_end_t = jnp.exp(last_t - acum_t) * dt_t
    e_last_t = jnp.exp(last_t)
    e_acum = jnp.exp(acum)

    for g in range(SSM_GROUPS):
        b_g = bm[:, g * N:(g + 1) * N]
        c_g = cm[:, g * N:(g + 1) * N]
        cb = lax.dot_general(c_g, b_g, (((1,), (1,)), ((), ())), preferred_element_type=F32)
        for hh in range(SSM_HEADS // SSM_GROUPS):
            h = g * (SSM_HEADS // SSM_GROUPS) + hh
            hs = slice(h * P, (h + 1) * P)
            decay = jnp.exp(jnp.where(causal, acum[:, h:h + 1] - acum_t[h:h + 1, :], -jnp.inf))
            mat = (cb * decay * dt_t[h:h + 1, :]).astype(BF16)
            y = jnp.dot(mat, xs_b[:, hs], preferred_element_type=F32)
            st = state_ref[hs, :]
            y = y + lax.dot_general(c_g, st.astype(BF16), (((1,), (1,)), ((), ())),
                                    preferred_element_type=F32) * e_acum[:, h:h + 1]
            y_ref[:, hs] = y
            xw = (xs_t[hs, :] * w_end_t[h:h + 1, :]).astype(BF16)
            state_ref[hs, :] = st * e_last_t[h:h + 1, :] + jnp.dot(xw, b_g, preferred_element_type=F32)

    y = y_ref[...] + dsk_ref[...] * xs
    y = y * _silu(z_ref[0])
    gw = SSM_WIDTH // SSM_GROUPS
    for g in range(SSM_GROUPS):
        cols = slice(g * gw, (g + 1) * gw)
        yg = y[:, cols]
        yg = yg * lax.rsqrt(jnp.mean(yg * yg, axis=-1, keepdims=True) + NORM_EPS)
        o_ref[0, :, cols] = (yg * nw_ref[:, cols]).astype(BF16)


def _ssd(proj, dt_raw, conv_w, conv_b, dt_bias, a_log, d_skip, ssm_norm):
    bsz, seq, _ = proj.shape
    L = SSM_CHUNK
    vec = lambda n: pl.BlockSpec((1, n), lambda b, c: (0, 0))
    return pl.pallas_call(
        _ssd_kernel,
        out_shape=jax.ShapeDtypeStruct((bsz, seq, SSM_WIDTH), BF16),
        grid=(bsz, seq // L),
        in_specs=[
            pl.BlockSpec((1, L, SSM_CONV_DIM), lambda b, c: (b, c, 0)),
            pl.BlockSpec((1, L, SSM_WIDTH), lambda b, c: (b, c, OFF_Z // SSM_WIDTH)),
            pl.BlockSpec((1, L, LANES), lambda b, c: (b, c, 0)),
            pl.BlockSpec((SSM_CONV, SSM_CONV_DIM), lambda b, c: (0, 0)),
            vec(SSM_CONV_DIM), vec(LANES), vec(LANES), vec(SSM_WIDTH), vec(SSM_WIDTH),
        ],
        out_specs=pl.BlockSpec((1, L, SSM_WIDTH), lambda b, c: (b, c, 0)),
        scratch_shapes=[
            pltpu.VMEM((L, SSM_CONV_DIM), F32),
            pltpu.VMEM((SSM_WIDTH, SSM_STATE), F32),
            pltpu.VMEM((L, SSM_WIDTH), F32),
        ],
        compiler_params=_params(("parallel", "arbitrary")),
        name="ssd",
    )(proj, proj, dt_raw, conv_w, conv_b, dt_bias, a_log, d_skip, ssm_norm)


def _route(logits):
    lane = lax.broadcasted_iota(jnp.int32, logits.shape, 1).astype(F32)
    big = float(LANES)

    def first_argmax(vals, mask):
        top = jnp.max(vals, axis=-1, keepdims=True)
        idx = jnp.min(jnp.where(mask & (vals >= top), lane, big), axis=-1, keepdims=True)
        return top, idx

    def masked_softmax(mask):
        z = jnp.where(mask, logits, -jnp.inf)
        e = jnp.exp(z - jnp.max(z, axis=-1, keepdims=True))
        return e / jnp.sum(e, axis=-1, keepdims=True)

    gmask = lane < MOE_GROUPS
    g_prob, g_idx = first_argmax(jnp.where(gmask, masked_softmax(gmask), -1.0), gmask)
    lo = MOE_GROUPS + g_idx * MOE_EPG
    fmask = (lane >= lo) & (lane < lo + MOE_EPG)
    p_exp = jnp.where(fmask, masked_softmax(fmask), -1.0)
    p1, i1 = first_argmax(p_exp, fmask)
    rest = fmask & (lane != i1)
    p2, i2 = first_argmax(jnp.where(rest, p_exp, -1.0), rest)
    den = p1 + p2
    out = jnp.where(lane == 0, i1 - MOE_GROUPS, 0.0)
    out = jnp.where(lane == 1, i2 - MOE_GROUPS, out)
    out = jnp.where(lane == 2, g_prob * p1 / den, out)
    out = jnp.where(lane == 3, g_prob * p2 / den, out)
    return out


def _outproj_kernel(ya_ref, yb_ref, yc_ref, x_ref, g1_ref, wa_ref, wb_ref, wc_ref,
                    n2_ref, sc_ref, sh_ref, wr_ref, br_ref, xo_ref, hc_ref, rt_ref):
    mix = jnp.dot(ya_ref[0], wa_ref[...], preferred_element_type=F32)
    mix = mix + jnp.dot(yb_ref[0], wb_ref[...], preferred_element_type=F32)
    mix = mix + jnp.dot(yc_ref[0], wc_ref[...], preferred_element_type=F32)
    xn = x_ref[0] + g1_ref[0] * mix
    xo_ref[0] = xn
    h = _modulated_norm(xn, n2_ref[...], sc_ref[0], sh_ref[0])
    half = N_PACK * LANES
    packed = _pack_halves(h[:, :half], h[:, half:])
    for c in range(N_PACK):
        hc_ref[0, c] = packed[:, c * LANES:(c + 1) * LANES]
    h_hi = h.astype(BF16)
    h_lo = (h - h_hi.astype(F32)).astype(BF16)
    part = jnp.dot(h_hi, wr_ref[...], preferred_element_type=F32)
    logits = (part[:, :LANES] + part[:, LANES:]
              + jnp.dot(h_lo, wr_ref[:, :LANES], preferred_element_type=F32) + br_ref[...])
    rt_ref[0] = _route(logits)


TM_TOK = 512


def _out_proj_route(ya, yb, yc, x, g1, wa, wb, wc, n2, sc2, sh2, w_r, b_r):
    bsz, seq, d = x.shape
    tm = TM_TOK
    nt = seq // tm
    row = lambda w: pl.BlockSpec((1, tm, w), lambda b, i: (b, i, 0))
    per_b = pl.BlockSpec((1, 1, d), lambda b, i: (b, 0, 0))
    const = lambda a: pl.BlockSpec(a.shape, lambda b, i: (0,) * a.ndim,
                                   pipeline_mode=pl.Buffered(1))
    return pl.pallas_call(
        _outproj_kernel,
        out_shape=(jax.ShapeDtypeStruct((bsz, seq, d), F32),
                   jax.ShapeDtypeStruct((bsz * nt, N_PACK, tm, LANES), U32),
                   jax.ShapeDtypeStruct((bsz, seq, LANES), F32)),
        grid=(bsz, nt),
        in_specs=[row(ya.shape[-1]), row(yb.shape[-1]), row(yc.shape[-1]), row(d), per_b,
                  const(wa), const(wb), const(wc), const(n2), per_b, per_b, const(w_r), const(b_r)],
        out_specs=(row(d),
                   pl.BlockSpec((1, N_PACK, tm, LANES), lambda b, i: (b * nt + i, 0, 0, 0)),
                   row(LANES)),
        compiler_params=_params(("parallel", "parallel")),
        name="out_proj_route",
    )(ya, yb, yc, x, g1, wa, wb, wc, n2, sc2, sh2, w_r, b_r)


BLOCK_SHIFT = MOE_BLOCK.bit_length() - 1


TOK_SHIFT = TM_TOK.bit_length() - 1
MXU_WIDTH = 256
GATHER_UNROLL = 8


def _expert_kernel(be_ref, nv_ref, tok_ref, h_hbm, wg_ref, wu_ref, wd_ref, o_ref,
                   buf_ref, x_ref, wgb_ref, wub_ref, wdb_ref, sem):
    i = pl.program_id(0)
    slot = i % 2
    half = N_PACK * LANES

    def gather(blk, slot_, mi):
        tok = tok_ref[blk * MOE_BLOCK + mi]
        return pltpu.make_async_copy(
            h_hbm.at[jnp.right_shift(tok, TOK_SHIFT), :, jnp.bitwise_and(tok, TM_TOK - 1), :],
            buf_ref.at[slot_, :, mi, :], sem.at[slot_])

    def for_valid_rows(blk, fn):
        def body(r, carry):
            for u in range(GATHER_UNROLL):
                fn(r * GATHER_UNROLL + u)
            return carry
        lax.fori_loop(0, (nv_ref[blk] + GATHER_UNROLL - 1) // GATHER_UNROLL, body, 0)

    @pl.when(i == 0)
    def _():
        buf_ref[...] = jnp.zeros_like(buf_ref)
        for_valid_rows(0, lambda mi: gather(0, 0, mi).start())

    @pl.when(i + 1 < pl.num_programs(0))
    def _():
        for_valid_rows(i + 1, lambda mi: gather(i + 1, 1 - slot, mi).start())

    @pl.when((i == 0) | (be_ref[i] != be_ref[jnp.maximum(i - 1, 0)]))
    def _():
        wgb_ref[...] = wg_ref[0, 0].astype(BF16)
        wub_ref[...] = wu_ref[0, 0].astype(BF16)
        wdb_ref[...] = wd_ref[0, 0].astype(BF16)

    for_valid_rows(i, lambda mi: gather(i, slot, mi).wait())

    @pl.when(nv_ref[i] == 0)
    def _():
        o_ref[...] = jnp.zeros_like(o_ref)

    @pl.when(nv_ref[i] > 0)
    def _():
        for c in range(N_PACK):
            lo, hi = _unpack_halves(buf_ref[slot, c])
            x_ref[:, c * LANES:(c + 1) * LANES] = lo.astype(BF16)
            x_ref[:, half + c * LANES:half + (c + 1) * LANES] = hi.astype(BF16)
        ff = wgb_ref.shape[1]
        hid = []
        for p in range(ff // MXU_WIDTH):
            cols = slice(p * MXU_WIDTH, (p + 1) * MXU_WIDTH)
            gate = jnp.dot(x_ref[...], wgb_ref[:, cols], preferred_element_type=F32)
            up = jnp.dot(x_ref[...], wub_ref[:, cols], preferred_element_type=F32)
            hid.append((_silu(gate) * up).astype(BF16))
        hid = jnp.concatenate(hid, axis=-1)
        for p in range(half // MXU_WIDTH):
            cols = slice(p * MXU_WIDTH, (p + 1) * MXU_WIDTH)
            y_lo = jnp.dot(hid, wdb_ref[:, cols], preferred_element_type=F32)
            y_hi = jnp.dot(hid, wdb_ref[:, half + p * MXU_WIDTH:half + (p + 1) * MXU_WIDTH],
                           preferred_element_type=F32)
            packed = _pack_halves(y_lo, y_hi)
            for c in range(MXU_WIDTH // LANES):
                o_ref[0, p * (MXU_WIDTH // LANES) + c] = packed[:, c * LANES:(c + 1) * LANES]


def _experts(layer, blk_exp, blk_rows, row_tok, h_chunks, w_gate, w_up, w_down):
    n_blocks = blk_exp.shape[0]
    d, ff = w_gate.shape[2], w_gate.shape[3]
    return pl.pallas_call(
        _expert_kernel,
        out_shape=jax.ShapeDtypeStruct((n_blocks, N_PACK, MOE_BLOCK, LANES), U32),
        grid_spec=pltpu.PrefetchScalarGridSpec(
            num_scalar_prefetch=3,
            grid=(n_blocks,),
            in_specs=[pl.BlockSpec(memory_space=pl.ANY)] + [
                pl.BlockSpec(shape, lambda i, be, nv, rt: (layer, be[i], 0, 0))
                for shape in ((1, 1, d, ff), (1, 1, d, ff), (1, 1, ff, d))],
            out_specs=pl.BlockSpec((1, N_PACK, MOE_BLOCK, LANES), lambda i, be, nv, rt: (i, 0, 0, 0)),
            scratch_shapes=[pltpu.VMEM((2, N_PACK, MOE_BLOCK, LANES), U32),
                            pltpu.VMEM((MOE_BLOCK, d), BF16),
                            pltpu.VMEM((d, ff), BF16), pltpu.VMEM((d, ff), BF16),
                            pltpu.VMEM((ff, d), BF16), pltpu.SemaphoreType.DMA((2,))],
        ),
        compiler_params=_params(("arbitrary",)),
        name="moe_experts",
    )(blk_exp, blk_rows, row_tok, h_chunks, w_gate, w_up, w_down)


def _combine_kernel(dest_ref, y_hbm, x_ref, g2_ref, rt_ref, fn_ref, o_ref, buf_ref, sem, *, final):
    i = pl.program_id(0)
    tm = TM_TOK

    def gather(k, mi):
        d = dest_ref[(i * tm + mi) * 2 + k]
        return pltpu.make_async_copy(
            y_hbm.at[jnp.right_shift(d, BLOCK_SHIFT), :, jnp.bitwise_and(d, MOE_BLOCK - 1), :],
            buf_ref.at[k, :, mi, :], sem)

    def issue(mi, carry):
        for k in range(2):
            gather(k, mi).start()
        return carry

    def drain(mi, carry):
        for k in range(2):
            gather(k, mi).wait()
        return carry

    lax.fori_loop(0, tm, issue, 0)
    lax.fori_loop(0, tm, drain, 0)
    rt = rt_ref[...]
    gate0, gate1 = rt[:, 2:3], rt[:, 3:4]
    for c in range(N_PACK):
        halves = zip(_unpack_halves(buf_ref[0, c]), _unpack_halves(buf_ref[1, c]))
        for part, (y0, y1) in enumerate(halves):
            cols = slice((part * N_PACK + c) * LANES, (part * N_PACK + c + 1) * LANES)
            o_ref[:, cols] = x_ref[:, cols] + g2_ref[0, :, cols] * (y0 * gate0 + y1 * gate1)
    if final:
        x = o_ref[...]
        ms = jnp.mean(x * x, axis=-1, keepdims=True)
        o_ref[...] = (x * lax.rsqrt(ms + NORM_EPS)) * fn_ref[...]


def _combine(dest_flat, y_rows, x2d, g2, route2d, seq, final_g, final):
    n_tok, d = x2d.shape
    tm = TM_TOK
    per_seq = seq // tm
    return pl.pallas_call(
        functools.partial(_combine_kernel, final=final),
        out_shape=jax.ShapeDtypeStruct((n_tok, d), F32),
        grid_spec=pltpu.PrefetchScalarGridSpec(
            num_scalar_prefetch=1,
            grid=(n_tok // tm,),
            in_specs=[pl.BlockSpec(memory_space=pl.ANY),
                      pl.BlockSpec((tm, d), lambda i, dr: (i, 0)),
                      pl.BlockSpec((1, 1, d), lambda i, dr: (i // per_seq, 0, 0)),
                      pl.BlockSpec((tm, LANES), lambda i, dr: (i, 0)),
                      pl.BlockSpec((1, d), lambda i, dr: (0, 0))],
            out_specs=pl.BlockSpec((tm, d), lambda i, dr: (i, 0)),
            scratch_shapes=[pltpu.VMEM((2, N_PACK, tm, LANES), U32), pltpu.SemaphoreType.DMA],
        ),
        compiler_params=_params(("arbitrary",)),
        name="moe_combine",
    )(dest_flat, y_rows, x2d, g2, route2d, final_g)


def _dispatch_plan(route2d):
    n_tok = route2d.shape[0]
    n_asg = n_tok * 2
    flat_e = route2d[:, 0:2].astype(jnp.int32).reshape(n_asg)
    asg = jnp.arange(n_asg, dtype=jnp.int32)
    s_e, order = lax.sort((flat_e, asg), num_keys=1)
    ids = jnp.arange(MOE_EXPERTS, dtype=jnp.int32)
    start = jnp.sum(s_e[:, None] < ids[None, :], axis=0, dtype=jnp.int32)
    counts = jnp.sum(s_e[:, None] == ids[None, :], axis=0, dtype=jnp.int32)
    pcounts = (counts + MOE_BLOCK - 1) // MOE_BLOCK * MOE_BLOCK
    pend = jnp.cumsum(pcounts)
    pstart = pend - pcounts
    dest_sorted = pstart[s_e] + asg - start[s_e]
    _, dest = lax.sort((order, dest_sorted), num_keys=1)
    n_blocks = n_asg // MOE_BLOCK + MOE_EXPERTS
    blk_start = jnp.arange(n_blocks, dtype=jnp.int32) * MOE_BLOCK
    blk_exp = jnp.minimum(jnp.sum(pend[None, :] <= blk_start[:, None], axis=1, dtype=jnp.int32),
                          MOE_EXPERTS - 1)
    row = jnp.arange(n_blocks * MOE_BLOCK, dtype=jnp.int32)
    row_e = jnp.repeat(blk_exp, MOE_BLOCK)
    off = row - pstart[row_e]
    src = jnp.minimum(start[row_e] + off, n_asg - 1)
    row_tok = jnp.where(off < counts[row_e], order[src] // 2, 0).astype(jnp.int32)
    blk_rows = jnp.clip(counts[blk_exp] - (blk_start - pstart[blk_exp]), 0, MOE_BLOCK).astype(jnp.int32)
    return dest, row_tok, blk_exp, blk_rows


def _rotary_tables(seq):
    pos = jnp.arange(seq, dtype=F32)
    inv_freq = ROPE_THETA ** (-jnp.arange(0, ROPE_DIM, 2, dtype=F32) / ROPE_DIM)
    ang = pos[:, None] * inv_freq[None, :]
    cos, sin = jnp.cos(ang), jnp.sin(ang)
    half = ROPE_DIM // 2
    ones = jnp.ones((seq, LANES - ROPE_DIM), F32)
    tab_c = jnp.concatenate([cos, cos, ones], axis=-1)
    tab_s1 = jnp.concatenate([-sin, jnp.zeros((seq, LANES - half), F32)], axis=-1)
    tab_s2 = jnp.concatenate([jnp.zeros((seq, half), F32), sin,
                              jnp.zeros((seq, LANES - ROPE_DIM), F32)], axis=-1)
    return tab_c, tab_s1, tab_s2


def _pad_lanes(v):
    return jnp.pad(v, (0, LANES - v.shape[0])).reshape(1, LANES)


def kernel(x, c, norm1, norm2, w_ada, b_ada, w_in, w_out, sgu_ln_g, sgu_ln_b, sgu_w, sgu_b, conv_w, conv_b, dt_bias, a_log, d_skip, ssm_norm, w_coarse, b_coarse, w_fine, b_fine, w_gate, w_up, w_down, final_norm):
    bsz, seq, d = x.shape
    depth = w_in.shape[0]
    n_tok = bsz * seq

    c_pad = jnp.pad(c, ((0, 8 - bsz % 8 if bsz % 8 else 0), (0, 0)))
    mod = _ada_mod(c_pad, w_ada, b_ada)[:, :bsz].reshape(depth, bsz, 6, 1, d)
    tab_c, tab_s1, tab_s2 = _rotary_tables(seq)

    for l in range(depth):
        sh1, sc1, g1, sh2, sc2, g2 = [mod[l, :, i] for i in range(6)]
        wl = w_in[l]
        w_main = jnp.concatenate(
            [wl[:, 3584:5120], wl[:, 0:512], wl[:, 2560:3584], wl[:, 512:1024], wl[:, 1024:2560]],
            axis=1).astype(BF16)
        w_dt = jnp.pad(wl[:, 5120:], ((0, 0), (0, LANES - SSM_HEADS))).astype(BF16)
        proj, dt_raw = _in_proj(x, norm1[l].reshape(1, d), sc1, sh1, w_main, w_dt)

        ya = _sgu(proj, sgu_ln_g[l].reshape(1, -1), sgu_ln_b[l].reshape(1, -1), sgu_w[l], sgu_b[l].T)
        qt, kr, vt, km = _rope_prep(proj, tab_c, tab_s1, tab_s2)
        yb = _moba(qt, kr, vt, km.reshape(km.shape[:3] + (ATT_HEAD_DIM,)))
        yc = _ssd(proj, dt_raw, conv_w[l], conv_b[l].reshape(1, -1), _pad_lanes(dt_bias[l]),
                  _pad_lanes(a_log[l]), jnp.repeat(d_skip[l], SSM_HEAD_DIM).reshape(1, -1),
                  ssm_norm[l].reshape(1, -1))

        wo = w_out[l].astype(BF16)
        w_r = jnp.concatenate([w_coarse[l], w_fine[l].transpose(1, 0, 2).reshape(d, MOE_EXPERTS)], axis=1)
        w_r = jnp.pad(w_r, ((0, 0), (0, LANES - w_r.shape[1])))
        w_r_hi = w_r.astype(BF16)
        w_r = jnp.concatenate([w_r_hi, (w_r - w_r_hi.astype(F32)).astype(BF16)], axis=1)
        b_r = _pad_lanes(jnp.concatenate([b_coarse[l], b_fine[l].reshape(-1)]))
        x, h_chunks, route = _out_proj_route(
            ya, yb, yc, x, g1, wo[:SGU_WIDTH], wo[SGU_WIDTH:2 * SGU_WIDTH], wo[2 * SGU_WIDTH:],
            norm2[l].reshape(1, d), sc2, sh2, w_r, b_r)

        route2d = route.reshape(n_tok, LANES)
        dest, row_tok, blk_exp, blk_rows = _dispatch_plan(route2d)
        y_rows = _experts(l, blk_exp, blk_rows, row_tok, h_chunks, w_gate, w_up, w_down)
        x = _combine(dest, y_rows, x.reshape(n_tok, d), g2, route2d, seq,
                     final_norm.reshape(1, d), final=(l == depth - 1)).reshape(bsz, seq, d)

    return x
```

```python
import functools

import jax
import jax.numpy as jnp
from jax import lax
from jax.experimental import pallas as pl
from jax.experimental.pallas import tpu as pltpu

F32 = jnp.float32
BF16 = jnp.bfloat16

LANES = 128
NORM_EPS = 1e-6
NEG_INF = -1e30

D_MODEL = 2048
N_CHUNK = D_MODEL // LANES
N_PACK = N_CHUNK // 2
SGU_WIDTH = 512
SGU_GROUPS = 4
SGU_CHUNK = 128
ATT_HEADS = 4
ATT_HEAD_DIM = 128
ROPE_DIM = 32
ROPE_THETA = 500000.0
MOBA_BLOCK = 256
MOBA_TOPK = 3
SSM_WIDTH = 1024
SSM_HEAD_DIM = 64
SSM_HEADS = 16
SSM_GROUPS = 2
SSM_STATE = 128
SSM_CONV = 4
SSM_CHUNK = 128
SSM_CONV_DIM = 1536
MOE_GROUPS = 4
MOE_EPG = 8
MOE_EXPERTS = 32
MOE_FF = 512
MOE_BLOCK = 512

PROJ_MAIN = 5120
OFF_U, OFF_Z, OFF_V, OFF_Q, OFF_K, OFF_VA = 1536, 2048, 3072, 3584, 4096, 4608

VMEM_LIMIT = 56 * 1024 * 1024


def _params(sem, vmem=VMEM_LIMIT):
    return pltpu.CompilerParams(dimension_semantics=sem, vmem_limit_bytes=vmem)


def _silu(x):
    return x * jax.nn.sigmoid(x)


U32 = jnp.uint32
HI_MASK = 0xFFFF0000


def _pack_halves(lo, hi):
    lo_bits = lax.bitcast_convert_type(lo.astype(BF16).astype(F32), U32)
    hi_bits = lax.bitcast_convert_type(hi.astype(BF16).astype(F32), U32)
    return jnp.bitwise_or(jnp.bitwise_and(hi_bits, U32(HI_MASK)), jnp.right_shift(lo_bits, U32(16)))


def _unpack_halves(w):
    lo = lax.bitcast_convert_type(jnp.left_shift(w, U32(16)), F32)
    hi = lax.bitcast_convert_type(jnp.bitwise_and(w, U32(HI_MASK)), F32)
    return lo, hi


def _ada_kernel(c_ref, w_ref, b_ref, o_ref):
    ca = _silu(c_ref[...]).astype(BF16)
    o_ref[0] = jnp.dot(ca, w_ref[0].astype(BF16), preferred_element_type=F32) + b_ref[0]


def _ada_mod(c_pad, w_ada, b_ada):
    depth, d, n = w_ada.shape
    tn = 1024
    return pl.pallas_call(
        _ada_kernel,
        out_shape=jax.ShapeDtypeStruct((depth, c_pad.shape[0], n), F32),
        grid=(depth, n // tn),
        in_specs=[
            pl.BlockSpec(c_pad.shape, lambda l, j: (0, 0)),
            pl.BlockSpec((1, d, tn), lambda l, j: (l, 0, j)),
            pl.BlockSpec((1, 1, tn), lambda l, j: (l, 0, j)),
        ],
        out_specs=pl.BlockSpec((1, c_pad.shape[0], tn), lambda l, j: (l, 0, j)),
        compiler_params=_params(("parallel", "parallel")),
        name="ada_mod",
    )(c_pad, w_ada, b_ada.reshape(depth, 1, n))


def _modulated_norm(x, g, sc, sh):
    ms = jnp.mean(x * x, axis=-1, keepdims=True)
    return (x * lax.rsqrt(ms + NORM_EPS)) * g * (1.0 + sc) + sh


def _inproj_kernel(x_ref, g_ref, sc_ref, sh_ref, w_ref, wdt_ref, o_ref, dt_ref, h_ref):
    @pl.when(pl.program_id(2) == 0)
    def _():
        h = _modulated_norm(x_ref[0], g_ref[...], sc_ref[0], sh_ref[0]).astype(BF16)
        h_ref[...] = h
        dt_ref[0] = jnp.dot(h, wdt_ref[...], preferred_element_type=F32)

    o_ref[0] = jnp.dot(h_ref[...], w_ref[...], preferred_element_type=F32)


def _in_proj(x, g, sc, sh, w, wdt):
    bsz, seq, d = x.shape
    n = w.shape[1]
    tm, tn = 512, 1024
    return pl.pallas_call(
        _inproj_kernel,
        out_shape=(jax.ShapeDtypeStruct((bsz, seq, n), F32),
                   jax.ShapeDtypeStruct((bsz, seq, LANES), F32)),
        grid=(bsz, seq // tm, n // tn),
        in_specs=[
            pl.BlockSpec((1, tm, d), lambda b, i, j: (b, i, 0)),
            pl.BlockSpec((1, d), lambda b, i, j: (0, 0)),
            pl.BlockSpec((1, 1, d), lambda b, i, j: (b, 0, 0)),
            pl.BlockSpec((1, 1, d), lambda b, i, j: (b, 0, 0)),
            pl.BlockSpec((d, tn), lambda b, i, j: (0, j)),
            pl.BlockSpec((d, LANES), lambda b, i, j: (0, 0)),
        ],
        out_specs=(pl.BlockSpec((1, tm, tn), lambda b, i, j: (b, i, j)),
                   pl.BlockSpec((1, tm, LANES), lambda b, i, j: (b, i, 0))),
        scratch_shapes=[pltpu.VMEM((tm, d), BF16)],
        compiler_params=_params(("parallel", "parallel", "arbitrary")),
        name="in_proj",
    )(x, g, sc, sh, w, wdt)


def _sgu_kernel(u_ref, v_ref, g_ref, b_ref, w_ref, bs_ref, o_ref, *, n_chunks):
    row = lax.broadcasted_iota(jnp.int32, (SGU_CHUNK, SGU_CHUNK), 0)
    col = lax.broadcasted_iota(jnp.int32, (SGU_CHUNK, SGU_CHUNK), 1)
    w_causal = [jnp.where(row >= col, w_ref[g], 0.0).astype(BF16) for g in range(SGU_GROUPS)]
    for ci in range(n_chunks):
        rows = slice(ci * SGU_CHUNK, (ci + 1) * SGU_CHUNK)
        u = jax.nn.gelu(u_ref[0, rows, :])
        vf = jax.nn.gelu(v_ref[0, rows, :])
        mu = jnp.mean(vf, axis=-1, keepdims=True)
        var = jnp.mean(jnp.square(vf - mu), axis=-1, keepdims=True)
        vn = ((vf - mu) * lax.rsqrt(var + NORM_EPS) * g_ref[...] + b_ref[...]).astype(BF16)
        for g in range(SGU_GROUPS):
            cols = slice(g * LANES, (g + 1) * LANES)
            s = jnp.dot(w_causal[g], vn[:, cols], preferred_element_type=F32) + bs_ref[:, g:g + 1]
            o_ref[0, rows, cols] = (u[:, cols] * s).astype(BF16)


def _sgu(proj, ln_g, ln_b, w_s, b_s_t):
    bsz, seq, _ = proj.shape
    tc = 512
    blk = lambda off: pl.BlockSpec((1, tc, SGU_WIDTH), lambda b, i: (b, i, off // SGU_WIDTH))
    return pl.pallas_call(
        functools.partial(_sgu_kernel, n_chunks=tc // SGU_CHUNK),
        out_shape=jax.ShapeDtypeStruct((bsz, seq, SGU_WIDTH), BF16),
        grid=(bsz, seq // tc),
        in_specs=[
            blk(OFF_U), blk(OFF_V),
            pl.BlockSpec((1, SGU_WIDTH), lambda b, i: (0, 0)),
            pl.BlockSpec((1, SGU_WIDTH), lambda b, i: (0, 0)),
            pl.BlockSpec((SGU_GROUPS, SGU_CHUNK, SGU_CHUNK), lambda b, i: (0, 0, 0)),
            pl.BlockSpec((SGU_CHUNK, SGU_GROUPS), lambda b, i: (0, 0)),
        ],
        out_specs=pl.BlockSpec((1, tc, SGU_WIDTH), lambda b, i: (b, i, 0)),
        compiler_params=_params(("parallel", "parallel")),
        name="sgu",
    )(proj, proj, ln_g, ln_b, w_s, b_s_t)


def _rotary(t, c, s1, s2):
    half = ROPE_DIM // 2
    return t * c + pltpu.roll(t, LANES - half, 1) * s1 + pltpu.roll(t, half, 1) * s2


def _rope_kernel(q_ref, k_ref, v_ref, c_ref, s1_ref, s2_ref, qt_ref, kr_ref, vt_ref, km_ref):
    c, s1, s2 = c_ref[...], s1_ref[...], s2_ref[...]
    for h in range(ATT_HEADS):
        cols = slice(h * ATT_HEAD_DIM, (h + 1) * ATT_HEAD_DIM)
        q = _rotary(q_ref[0, :, cols], c, s1, s2)
        k = _rotary(k_ref[0, :, cols], c, s1, s2)
        qt_ref[0, h, 0] = q.T
        kr_ref[0, h, 0] = k.astype(BF16)
        vt_ref[0, h, 0] = v_ref[0, :, cols].T.astype(BF16)
        km_ref[0, h, 0] = jnp.mean(k, axis=0, keepdims=True)


def _rope_prep(proj, tab_c, tab_s1, tab_s2):
    bsz, seq, _ = proj.shape
    nblk = seq // MOBA_BLOCK
    hd, nh = ATT_HEAD_DIM, ATT_HEADS
    width = nh * hd
    sec = lambda off: pl.BlockSpec((1, MOBA_BLOCK, width), lambda i, b: (b, i, off // width))
    tab = pl.BlockSpec((MOBA_BLOCK, hd), lambda i, b: (i, 0))
    out = lambda rows, cols: pl.BlockSpec((1, nh, 1, rows, cols), lambda i, b: (b, 0, i, 0, 0))
    return pl.pallas_call(
        _rope_kernel,
        out_shape=(
            jax.ShapeDtypeStruct((bsz, nh, nblk, hd, MOBA_BLOCK), F32),
            jax.ShapeDtypeStruct((bsz, nh, nblk, MOBA_BLOCK, hd), BF16),
            jax.ShapeDtypeStruct((bsz, nh, nblk, hd, MOBA_BLOCK), BF16),
            jax.ShapeDtypeStruct((bsz, nh, nblk, 1, hd), F32),
        ),
        grid=(nblk, bsz),
        in_specs=[sec(OFF_Q), sec(OFF_K), sec(OFF_VA), tab, tab, tab],
        out_specs=(out(hd, MOBA_BLOCK), out(MOBA_BLOCK, hd), out(hd, MOBA_BLOCK), out(1, hd)),
        compiler_params=_params(("parallel", "parallel")),
        name="rope_prep",
    )(proj, proj, proj, tab_c, tab_s1, tab_s2)


LOG2E = 1.4426950408889634
SOFTMAX_ROWS = 64


def _moba_kernel(qt_ref, kr_ref, vt_ref, km_ref, o_ref, sel_ref, qs_ref, acc_ref, p_ref, s_ref, *, nblk):
    i = pl.program_id(1)
    tq, nh, hd = MOBA_BLOCK, ATT_HEADS, ATT_HEAD_DIM
    blk = lax.broadcasted_iota(jnp.int32, (nblk, tq), 0)
    valid = blk < i
    blk_f = blk.astype(F32)
    krow =lax.broadcasted_iota(jnp.int32, (tq, tq), 0)
    qcol = lax.broadcasted_iota(jnp.int32, (tq, tq), 1)

    stats = []
    for h in range(nh):
        qt = qt_ref[0, h, 0]
        qs_ref[h] = (qt * (hd ** -0.5 * LOG2E)).astype(BF16)

        gate = jnp.dot(km_ref[0, h], qt, preferred_element_type=F32,
                       precision=lax.Precision.HIGHEST)
        gate = jnp.where(valid, gate, NEG_INF)
        picked = jnp.zeros((nblk, tq), jnp.bool_)
        for _ in range(MOBA_TOPK):
            top = jnp.max(gate, axis=0, keepdims=True)
            first = jnp.min(jnp.where(gate == top, blk_f, float(nblk)), axis=0, keepdims=True)
            pick = blk_f == first
            picked = picked | pick
            gate = jnp.where(pick, -jnp.inf, gate)
        sel_ref[h] = jnp.where(valid & picked, 1.0, 0.0)

        s = jnp.dot(kr_ref[0, h, i], qs_ref[h], preferred_element_type=F32)
        s = jnp.where(krow <= qcol, s, NEG_INF)
        m0 = jnp.max(s, axis=0, keepdims=True)
        p = jnp.exp2(s - m0)
        acc_ref[h] = jnp.dot(vt_ref[0, h, i], p.astype(BF16), preferred_element_type=F32)
        stats += [m0, jnp.sum(p, axis=0, keepdims=True)]

    def put_scores(slot, j):
        for h in range(nh):
            s = jnp.dot(kr_ref[0, h, j], qs_ref[h], preferred_element_type=F32)
            s_ref[slot, h] = jnp.where(sel_ref[h, pl.ds(j, 1), :] > 0.5, s, NEG_INF)

    def consume(slot, j, stats):
        out = []
        for h in range(nh):
            m, l = stats[2 * h], stats[2 * h + 1]
            m_new = jnp.maximum(m, jnp.max(s_ref[slot, h], axis=0, keepdims=True))
            alpha = jnp.exp2(m - m_new)
            l = alpha * l
            for r in range(0, tq, SOFTMAX_ROWS):
                p = jnp.exp2(s_ref[slot, h, r:r + SOFTMAX_ROWS, :] - m_new)
                l = l + jnp.sum(p, axis=0, keepdims=True)
                p_ref[h, r:r + SOFTMAX_ROWS, :] = p.astype(BF16)
            acc_ref[h] = alpha * acc_ref[h] + jnp.dot(vt_ref[0, h, j], p_ref[h],
                                                      preferred_element_type=F32)
            out += [m_new, l]
        return out

    put_scores(0, 0)

    def body(t, stats):
        first = 2 * t
        second = jnp.minimum(first + 1, nblk - 1)
        put_scores(1, second)
        stats = consume(0, first, stats)
        put_scores(0, jnp.minimum(first + 2, nblk - 1))
        return tuple(consume(1, second, stats))

    stats = lax.fori_loop(0, (i + 1) // 2, body, tuple(stats))
    for h in range(nh):
        o_ref[0, :, h * hd:(h + 1) * hd] = (acc_ref[h] / stats[2 * h + 1]).T.astype(BF16)


def _moba(qt, kr, vt, km):
    bsz, nh, nblk, hd, tq = qt.shape
    seq = nblk * tq
    full = lambda shape: pl.BlockSpec((1, nh) + shape, lambda b, i: (b, 0) + (0,) * len(shape))
    return pl.pallas_call(
        functools.partial(_moba_kernel, nblk=nblk),
        out_shape=jax.ShapeDtypeStruct((bsz, seq, nh * hd), BF16),
        grid=(bsz, nblk),
        in_specs=[
            pl.BlockSpec((1, nh, 1, hd, tq), lambda b, i: (b, 0, i, 0, 0)),
            full((nblk, tq, hd)), full((nblk, hd, tq)), full((nblk, hd)),
        ],
        out_specs=pl.BlockSpec((1, tq, nh * hd), lambda b, i: (b, i, 0)),
        scratch_shapes=[pltpu.VMEM((nh, nblk, tq), F32), pltpu.VMEM((nh, hd, tq), BF16),
                        pltpu.VMEM((nh, hd, tq), F32), pltpu.VMEM((nh, tq, tq), BF16),
                        pltpu.VMEM((2, nh, tq, tq), F32)],
        compiler_params=_params(("parallel", "arbitrary")),
        name="moba",
    )(qt, kr, vt, km)


def _softplus(x):
    return jnp.maximum(x, 0.0) + jnp.log1p(jnp.exp(-jnp.abs(x)))


def _ssd_kernel(xbc_ref, z_ref, dt_ref, cw_ref, cb_ref, dtb_ref, alog_ref, dsk_ref, nw_ref,
                o_ref, prev_ref, state_ref, y_ref):
    L, P, N = SSM_CHUNK, SSM_HEAD_DIM, SSM_STATE
    hi = lax.Precision.HIGHEST

    @pl.when(pl.program_id(1) == 0)
    def _():
        prev_ref[...] = jnp.zeros_like(prev_ref)
        state_ref[...] = jnp.zeros_like(state_ref)

    cur = xbc_ref[0]
    prev = prev_ref[...]
    crow = lax.broadcasted_iota(jnp.int32, cur.shape, 0)
    acc = cur * cw_ref[SSM_CONV - 1:SSM_CONV, :] + cb_ref[...]
    for j in range(1, SSM_CONV):
        shifted = pltpu.roll(jnp.where(crow >= L - j, prev, cur), j, 0)
        acc = acc + shifted * cw_ref[SSM_CONV - 1 - j:SSM_CONV - j, :]
    prev_ref[...] = cur
    xc = _silu(acc)
    xs = xc[:, :SSM_WIDTH]
    bm = xc[:, SSM_WIDTH:SSM_WIDTH + SSM_GROUPS * N].astype(BF16)
    cm = xc[:, SSM_WIDTH + SSM_GROUPS * N:].astype(BF16)
    xs_b = xs.astype(BF16)
    xs_t = xs.T

    dt = _softplus(dt_ref[0] + dtb_ref[...])
    da = dt * (-jnp.exp(alog_ref[...]))
    row = lax.broadcasted_iota(jnp.int32, (L, L), 0)
    col = lax.broadcasted_iota(jnp.int32, (L, L), 1)
    causal = row >= col
    acum = jnp.dot(jnp.where(causal, 1.0, 0.0), da, preferred_element_type=F32, precision=hi)
    acum_t = jnp.dot(da.T, jnp.where(row <= col, 1.0, 0.0), preferred_element_type=F32, precision=hi)
    dt_t = dt.T
    last_t = acum_t[:, L - 1:L]
    w_end_t = jnp.exp(last_t - acum_t) * dt_t
    e_last_t = jnp.exp(last_t)
    e_acum = jnp.exp(acum)

    for g in range(SSM_GROUPS):
        b_g = bm[:, g * N:(g + 1) * N]
        c_g = cm[:, g * N:(g + 1) * N]
        cb = lax.dot_general(c_g, b_g, (((1,), (1,)), ((), ())), preferred_element_type=F32)
        for hh in range(SSM_HEADS // SSM_GROUPS):
            h = g * (SSM_HEADS // SSM_GROUPS) + hh
            hs = slice(h * P, (h + 1) * P)
            decay = jnp.exp(jnp.where(causal, acum[:, h:h + 1] - acum_t[h:h + 1, :], -jnp.inf))
            mat = (cb * decay * dt_t[h:h + 1, :]).astype(BF16)
            y = jnp.dot(mat, xs_b[:, hs], preferred_element_type=F32)
            st = state_ref[hs, :]
            y = y + lax.dot_general(c_g, st.astype(BF16), (((1,), (1,)), ((), ())),
                                    preferred_element_type=F32) * e_acum[:, h:h + 1]
            y_ref[:, hs] = y
            xw = (xs_t[hs, :] * w_end_t[h:h + 1, :]).astype(BF16)
            state_ref[hs, :] = st * e_last_t[h:h + 1, :] + jnp.dot(xw, b_g, preferred_element_type=F32)

    y = y_ref[...] + dsk_ref[...] * xs
    y = y * _silu(z_ref[0])
    gw = SSM_WIDTH // SSM_GROUPS
    for g in range(SSM_GROUPS):
        cols = slice(g * gw, (g + 1) * gw)
        yg = y[:, cols]
        yg = yg * lax.rsqrt(jnp.mean(yg * yg, axis=-1, keepdims=True) + NORM_EPS)
        o_ref[0, :, cols] = (yg * nw_ref[:, cols]).astype(BF16)


def _ssd(proj, dt_raw, conv_w, conv_b, dt_bias, a_log, d_skip, ssm_norm):
    bsz, seq, _ = proj.shape
    L = SSM_CHUNK
    vec = lambda n: pl.BlockSpec((1, n), lambda b, c: (0, 0))
    return pl.pallas_call(
        _ssd_kernel,
        out_shape=jax.ShapeDtypeStruct((bsz, seq, SSM_WIDTH), BF16),
        grid=(bsz, seq // L),
        in_specs=[
            pl.BlockSpec((1, L, SSM_CONV_DIM), lambda b, c: (b, c, 0)),
            pl.BlockSpec((1, L, SSM_WIDTH), lambda b, c: (b, c, OFF_Z // SSM_WIDTH)),
            pl.BlockSpec((1, L, LANES), lambda b, c: (b, c, 0)),
            pl.BlockSpec((SSM_CONV, SSM_CONV_DIM), lambda b, c: (0, 0)),
            vec(SSM_CONV_DIM), vec(LANES), vec(LANES), vec(SSM_WIDTH), vec(SSM_WIDTH),
        ],
        out_specs=pl.BlockSpec((1, L, SSM_WIDTH), lambda b, c: (b, c, 0)),
        scratch_shapes=[
            pltpu.VMEM((L, SSM_CONV_DIM), F32),
            pltpu.VMEM((SSM_WIDTH, SSM_STATE), F32),
            pltpu.VMEM((L, SSM_WIDTH), F32),
        ],
        compiler_params=_params(("parallel", "arbitrary")),
        name="ssd",
    )(proj, proj, dt_raw, conv_w, conv_b, dt_bias, a_log, d_skip, ssm_norm)


def _route(logits):
    lane = lax.broadcasted_iota(jnp.int32, logits.shape, 1).astype(F32)
    big = float(LANES)

    def first_argmax(vals, mask):
        top = jnp.max(vals, axis=-1, keepdims=True)
        idx = jnp.min(jnp.where(mask & (vals >= top), lane, big), axis=-1, keepdims=True)
        return top, idx

    def masked_softmax(mask):
        z = jnp.where(mask, logits, -jnp.inf)
        e = jnp.exp(z - jnp.max(z, axis=-1, keepdims=True))
        return e / jnp.sum(e, axis=-1, keepdims=True)

    gmask = lane < MOE_GROUPS
    g_prob, g_idx = first_argmax(jnp.where(gmask, masked_softmax(gmask), -1.0), gmask)
    lo = MOE_GROUPS + g_idx * MOE_EPG
    fmask = (lane >= lo) & (lane < lo + MOE_EPG)
    p_exp = jnp.where(fmask, masked_softmax(fmask), -1.0)
    p1, i1 = first_argmax(p_exp, fmask)
    rest = fmask & (lane != i1)
    p2, i2 = first_argmax(jnp.where(rest, p_exp, -1.0), rest)
    den = p1 + p2
    out = jnp.where(lane == 0, i1 - MOE_GROUPS, 0.0)
    out = jnp.where(lane == 1, i2 - MOE_GROUPS, out)
    out = jnp.where(lane == 2, g_prob * p1 / den, out)
    out = jnp.where(lane == 3, g_prob * p2 / den, out)
    return out


def _outproj_kernel(ya_ref, yb_ref, yc_ref, x_ref, g1_ref, wa_ref, wb_ref, wc_ref,
                    n2_ref, sc_ref, sh_ref, wr_ref, br_ref, xo_ref, hc_ref, rt_ref):
    mix = jnp.dot(ya_ref[0], wa_ref[...], preferred_element_type=F32)
    mix = mix + jnp.dot(yb_ref[0], wb_ref[...], preferred_element_type=F32)
    mix = mix + jnp.dot(yc_ref[0], wc_ref[...], preferred_element_type=F32)
    xn = x_ref[0] + g1_ref[0] * mix
    xo_ref[0] = xn
    h = _modulated_norm(xn, n2_ref[...], sc_ref[0], sh_ref[0])
    half = N_PACK * LANES
    packed = _pack_halves(h[:, :half], h[:, half:])
    for c in range(N_PACK):
        hc_ref[0, c] = packed[:, c * LANES:(c + 1) * LANES]
    h_hi = h.astype(BF16)
    h_lo = (h - h_hi.astype(F32)).astype(BF16)
    part = jnp.dot(h_hi, wr_ref[...], preferred_element_type=F32)
    logits = (part[:, :LANES] + part[:, LANES:]
              + jnp.dot(h_lo, wr_ref[:, :LANES], preferred_element_type=F32) + br_ref[...])
    rt_ref[0] = _route(logits)


TM_TOK = 512


def _out_proj_route(ya, yb, yc, x, g1, wa, wb, wc, n2, sc2, sh2, w_r, b_r):
    bsz, seq, d = x.shape
    tm = TM_TOK
    nt = seq // tm
    row = lambda w: pl.BlockSpec((1, tm, w), lambda b, i: (b, i, 0))
    per_b = pl.BlockSpec((1, 1, d), lambda b, i: (b, 0, 0))
    const = lambda a: pl.BlockSpec(a.shape, lambda b, i: (0,) * a.ndim,
                                   pipeline_mode=pl.Buffered(1))
    return pl.pallas_call(
        _outproj_kernel,
        out_shape=(jax.ShapeDtypeStruct((bsz, seq, d), F32),
                   jax.ShapeDtypeStruct((bsz * nt, N_PACK, tm, LANES), U32),
                   jax.ShapeDtypeStruct((bsz, seq, LANES), F32)),
        grid=(bsz, nt),
        in_specs=[row(ya.shape[-1]), row(yb.shape[-1]), row(yc.shape[-1]), row(d), per_b,
                  const(wa), const(wb), const(wc), const(n2), per_b, per_b, const(w_r), const(b_r)],
        out_specs=(row(d),
                   pl.BlockSpec((1, N_PACK, tm, LANES), lambda b, i: (b * nt + i, 0, 0, 0)),
                   row(LANES)),
        compiler_params=_params(("parallel", "parallel")),
        name="out_proj_route",
    )(ya, yb, yc, x, g1, wa, wb, wc, n2, sc2, sh2, w_r, b_r)


BLOCK_SHIFT = MOE_BLOCK.bit_length() - 1


TOK_SHIFT = TM_TOK.bit_length() - 1
MXU_WIDTH = 256
GATHER_UNROLL = 8


def _expert_kernel(be_ref, nv_ref, tok_ref, h_hbm, wg_ref, wu_ref, wd_ref, o_ref,
                   buf_ref, x_ref, wgb_ref, wub_ref, wdb_ref, sem):
    i = pl.program_id(0)
    slot = i % 2
    half = N_PACK * LANES

    def gather(blk, slot_, mi):
        tok = tok_ref[blk * MOE_BLOCK + mi]
        return pltpu.make_async_copy(
            h_hbm.at[jnp.right_shift(tok, TOK_SHIFT), :, jnp.bitwise_and(tok, TM_TOK - 1), :],
            buf_ref.at[slot_, :, mi, :], sem.at[slot_])

    def for_valid_rows(blk, fn):
        def body(r, carry):
            for u in range(GATHER_UNROLL):
                fn(r * GATHER_UNROLL + u)
            return carry
        lax.fori_loop(0, (nv_ref[blk] + GATHER_UNROLL - 1) // GATHER_UNROLL, body, 0)

    @pl.when(i == 0)
    def _():
        buf_ref[...] = jnp.zeros_like(buf_ref)
        for_valid_rows(0, lambda mi: gather(0, 0, mi).start())

    @pl.when(i + 1 < pl.num_programs(0))
    def _():
        for_valid_rows(i + 1, lambda mi: gather(i + 1, 1 - slot, mi).start())

    @pl.when((i == 0) | (be_ref[i] != be_ref[jnp.maximum(i - 1, 0)]))
    def _():
        wgb_ref[...] = wg_ref[0, 0].astype(BF16)
        wub_ref[...] = wu_ref[0, 0].astype(BF16)
        wdb_ref[...] = wd_ref[0, 0].astype(BF16)

    for_valid_rows(i, lambda mi: gather(i, slot, mi).wait())

    @pl.when(nv_ref[i] == 0)
    def _():
        o_ref[...] = jnp.zeros_like(o_ref)

    @pl.when(nv_ref[i] > 0)
    def _():
        for c in range(N_PACK):
            lo, hi = _unpack_halves(buf_ref[slot, c])
            x_ref[:, c * LANES:(c + 1) * LANES] = lo.astype(BF16)
            x_ref[:, half + c * LANES:half + (c + 1) * LANES] = hi.astype(BF16)
        ff = wgb_ref.shape[1]
        hid = []
        for p in range(ff // MXU_WIDTH):
            cols = slice(p * MXU_WIDTH, (p + 1) * MXU_WIDTH)
            gate = jnp.dot(x_ref[...], wgb_ref[:, cols], preferred_element_type=F32)
            up = jnp.dot(x_ref[...], wub_ref[:, cols], preferred_element_type=F32)
            hid.append((_silu(gate) * up).astype(BF16))
        hid = jnp.concatenate(hid, axis=-1)
        for p in range(half // MXU_WIDTH):
            cols = slice(p * MXU_WIDTH, (p + 1) * MXU_WIDTH)
            y_lo = jnp.dot(hid, wdb_ref[:, cols], preferred_element_type=F32)
            y_hi = jnp.dot(hid, wdb_ref[:, half + p * MXU_WIDTH:half + (p + 1) * MXU_WIDTH],
                           preferred_element_type=F32)
            packed = _pack_halves(y_lo, y_hi)
            for c in range(MXU_WIDTH // LANES):
                o_ref[0, p * (MXU_WIDTH // LANES) + c] = packed[:, c * LANES:(c + 1) * LANES]


def _experts(layer, blk_exp, blk_rows, row_tok, h_chunks, w_gate, w_up, w_down):
    n_blocks = blk_exp.shape[0]
    d, ff = w_gate.shape[2], w_gate.shape[3]
    return pl.pallas_call(
        _expert_kernel,
        out_shape=jax.ShapeDtypeStruct((n_blocks, N_PACK, MOE_BLOCK, LANES), U32),
        grid_spec=pltpu.PrefetchScalarGridSpec(
            num_scalar_prefetch=3,
            grid=(n_blocks,),
            in_specs=[pl.BlockSpec(memory_space=pl.ANY)] + [
                pl.BlockSpec(shape, lambda i, be, nv, rt: (layer, be[i], 0, 0))
                for shape in ((1, 1, d, ff), (1, 1, d, ff), (1, 1, ff, d))],
            out_specs=pl.BlockSpec((1, N_PACK, MOE_BLOCK, LANES), lambda i, be, nv, rt: (i, 0, 0, 0)),
            scratch_shapes=[pltpu.VMEM((2, N_PACK, MOE_BLOCK, LANES), U32),
                            pltpu.VMEM((MOE_BLOCK, d), BF16),
                            pltpu.VMEM((d, ff), BF16), pltpu.VMEM((d, ff), BF16),
                            pltpu.VMEM((ff, d), BF16), pltpu.SemaphoreType.DMA((2,))],
        ),
        compiler_params=_params(("arbitrary",)),
        name="moe_experts",
    )(blk_exp, blk_rows, row_tok, h_chunks, w_gate, w_up, w_down)


def _combine_kernel(dest_ref, y_hbm, x_ref, g2_ref, rt_ref, fn_ref, o_ref, buf_ref, sem, *, final):
    i = pl.program_id(0)
    slot = i % 2
    tm = TM_TOK

    def gather(tile, slot_, k, mi):
        d = dest_ref[(tile * tm + mi) * 2 + k]
        return pltpu.make_async_copy(
            y_hbm.at[jnp.right_shift(d, BLOCK_SHIFT), :, jnp.bitwise_and(d, MOE_BLOCK - 1), :],
            buf_ref.at[slot_, k, :, mi, :], sem.at[slot_])

    def for_rows(fn):
        def body(r, carry):
            for u in range(GATHER_UNROLL // 2):
                for k in range(2):
                    fn(k, r * (GATHER_UNROLL // 2) + u)
            return carry
        lax.fori_loop(0, tm // (GATHER_UNROLL // 2), body, 0)

    @pl.when(i == 0)
    def _():
        for_rows(lambda k, mi: gather(0, 0, k, mi).start())

    @pl.when(i + 1 < pl.num_programs(0))
    def _():
        for_rows(lambda k, mi: gather(i + 1, 1 - slot, k, mi).start())

    for_rows(lambda k, mi: gather(i, slot, k, mi).wait())
    rt = rt_ref[...]
    gate0, gate1 = rt[:, 2:3], rt[:, 3:4]
    for c in range(N_PACK):
        halves = zip(_unpack_halves(buf_ref[slot, 0, c]), _unpack_halves(buf_ref[slot, 1, c]))
        for part, (y0, y1) in enumerate(halves):
            cols = slice((part * N_PACK + c) * LANES, (part * N_PACK + c + 1) * LANES)
            o_ref[:, cols] = x_ref[:, cols] + g2_ref[0, :, cols] * (y0 * gate0 + y1 * gate1)
    if final:
        x = o_ref[...]
        ms = jnp.mean(x * x, axis=-1, keepdims=True)
        o_ref[...] = (x * lax.rsqrt(ms + NORM_EPS)) * fn_ref[...]


def _combine(dest_flat, y_rows, x2d, g2, route2d, seq, final_g, final):
    n_tok, d = x2d.shape
    tm = TM_TOK
    per_seq = seq // tm
    return pl.pallas_call(
        functools.partial(_combine_kernel, final=final),
        out_shape=jax.ShapeDtypeStruct((n_tok, d), F32),
        grid_spec=pltpu.PrefetchScalarGridSpec(
            num_scalar_prefetch=1,
            grid=(n_tok // tm,),
            in_specs=[pl.BlockSpec(memory_space=pl.ANY),
                      pl.BlockSpec((tm, d), lambda i, dr: (i, 0)),
                      pl.BlockSpec((1, 1, d), lambda i, dr: (i // per_seq, 0, 0)),
                      pl.BlockSpec((tm, LANES), lambda i, dr: (i, 0)),
                      pl.BlockSpec((1, d), lambda i, dr: (0, 0))],
            out_specs=pl.BlockSpec((tm, d), lambda i, dr: (i, 0)),
            scratch_shapes=[pltpu.VMEM((2, 2, N_PACK, tm, LANES), U32),
                            pltpu.SemaphoreType.DMA((2,))],
        ),
        compiler_params=_params(("arbitrary",)),
        name="moe_combine",
    )(dest_flat, y_rows, x2d, g2, route2d, final_g)


def _dispatch_plan(route2d):
    n_tok = route2d.shape[0]
    n_asg = n_tok * 2
    flat_e = route2d[:, 0:2].astype(jnp.int32).reshape(n_asg)
    asg = jnp.arange(n_asg, dtype=jnp.int32)
    s_e, order = lax.sort((flat_e, asg), num_keys=1)
    ids = jnp.arange(MOE_EXPERTS, dtype=jnp.int32)
    start = jnp.sum(s_e[:, None] < ids[None, :], axis=0, dtype=jnp.int32)
    counts = jnp.sum(s_e[:, None] == ids[None, :], axis=0, dtype=jnp.int32)
    pcounts = (counts + MOE_BLOCK - 1) // MOE_BLOCK * MOE_BLOCK
    pend = jnp.cumsum(pcounts)
    pstart = pend - pcounts
    dest_sorted = pstart[s_e] + asg - start[s_e]
    _, dest = lax.sort((order, dest_sorted), num_keys=1)
    n_blocks = n_asg // MOE_BLOCK + MOE_EXPERTS
    blk_start = jnp.arange(n_blocks, dtype=jnp.int32) * MOE_BLOCK
    blk_exp = jnp.minimum(jnp.sum(pend[None, :] <= blk_start[:, None], axis=1, dtype=jnp.int32),
                          MOE_EXPERTS - 1)
    row = jnp.arange(n_blocks * MOE_BLOCK, dtype=jnp.int32)
    row_e = jnp.repeat(blk_exp, MOE_BLOCK)
    off = row - pstart[row_e]
    src = jnp.minimum(start[row_e] + off, n_asg - 1)
    row_tok = jnp.where(off < counts[row_e], order[src] // 2, 0).astype(jnp.int32)
    blk_rows = jnp.clip(counts[blk_exp] - (blk_start - pstart[blk_exp]), 0, MOE_BLOCK).astype(jnp.int32)
    return dest, row_tok, blk_exp, blk_rows


def _rotary_tables(seq):
    pos = jnp.arange(seq, dtype=F32)
    inv_freq = ROPE_THETA ** (-jnp.arange(0, ROPE_DIM, 2, dtype=F32) / ROPE_DIM)
    ang = pos[:, None] * inv_freq[None, :]
    cos, sin = jnp.cos(ang), jnp.sin(ang)
    half = ROPE_DIM // 2
    ones = jnp.ones((seq, LANES - ROPE_DIM), F32)
    tab_c = jnp.concatenate([cos, cos, ones], axis=-1)
    tab_s1 = jnp.concatenate([-sin, jnp.zeros((seq, LANES - half), F32)], axis=-1)
    tab_s2 = jnp.concatenate([jnp.zeros((seq, half), F32), sin,
                              jnp.zeros((seq, LANES - ROPE_DIM), F32)], axis=-1)
    return tab_c, tab_s1, tab_s2


def _pad_lanes(v):
    return jnp.pad(v, (0, LANES - v.shape[0])).reshape(1, LANES)


def kernel(x, c, norm1, norm2, w_ada, b_ada, w_in, w_out, sgu_ln_g, sgu_ln_b, sgu_w, sgu_b, conv_w, conv_b, dt_bias, a_log, d_skip, ssm_norm, w_coarse, b_coarse, w_fine, b_fine, w_gate, w_up, w_down, final_norm):
    bsz, seq, d = x.shape
    depth = w_in.shape[0]
    n_tok = bsz * seq

    c_pad = jnp.pad(c, ((0, 8 - bsz % 8 if bsz % 8 else 0), (0, 0)))
    mod = _ada_mod(c_pad, w_ada, b_ada)[:, :bsz].reshape(depth, bsz, 6, 1, d)
    tab_c, tab_s1, tab_s2 = _rotary_tables(seq)

    for l in range(depth):
        sh1, sc1, g1, sh2, sc2, g2 = [mod[l, :, i] for i in range(6)]
        wl = w_in[l]
        w_main = jnp.concatenate(
            [wl[:, 3584:5120], wl[:, 0:512], wl[:, 2560:3584], wl[:, 512:1024], wl[:, 1024:2560]],
            axis=1).astype(BF16)
        w_dt = jnp.pad(wl[:, 5120:], ((0, 0), (0, LANES - SSM_HEADS))).astype(BF16)
        proj, dt_raw = _in_proj(x, norm1[l].reshape(1, d), sc1, sh1, w_main, w_dt)

        ya = _sgu(proj, sgu_ln_g[l].reshape(1, -1), sgu_ln_b[l].reshape(1, -1), sgu_w[l], sgu_b[l].T)
        qt, kr, vt, km = _rope_prep(proj, tab_c, tab_s1, tab_s2)
        yb = _moba(qt, kr, vt, km.reshape(km.shape[:3] + (ATT_HEAD_DIM,)))
        yc = _ssd(proj, dt_raw, conv_w[l], conv_b[l].reshape(1, -1), _pad_lanes(dt_bias[l]),
                  _pad_lanes(a_log[l]), jnp.repeat(d_skip[l], SSM_HEAD_DIM).reshape(1, -1),
                  ssm_norm[l].reshape(1, -1))

        wo = w_out[l].astype(BF16)
        w_r = jnp.concatenate([w_coarse[l], w_fine[l].transpose(1, 0, 2).reshape(d, MOE_EXPERTS)], axis=1)
        w_r = jnp.pad(w_r, ((0, 0), (0, LANES - w_r.shape[1])))
        w_r_hi = w_r.astype(BF16)
        w_r = jnp.concatenate([w_r_hi, (w_r - w_r_hi.astype(F32)).astype(BF16)], axis=1)
        b_r = _pad_lanes(jnp.concatenate([b_coarse[l], b_fine[l].reshape(-1)]))
        x, h_chunks, route = _out_proj_route(
            ya, yb, yc, x, g1, wo[:SGU_WIDTH], wo[SGU_WIDTH:2 * SGU_WIDTH], wo[2 * SGU_WIDTH:],
            norm2[l].reshape(1, d), sc2, sh2, w_r, b_r)

        route2d = route.reshape(n_tok, LANES)
        dest, row_tok, blk_exp, blk_rows = _dispatch_plan(route2d)
        y_rows = _experts(l, blk_exp, blk_rows, row_tok, h_chunks, w_gate, w_up, w_down)
        x = _combine(dest, y_rows, x.reshape(n_tok, d), g2, route2d, seq,
                     final_norm.reshape(1, d), final=(l == depth - 1)).reshape(bsz, seq, d)

    return x
```

```python
import functools

import jax
import jax.numpy as jnp
from jax import lax
from jax.experimental import pallas as pl
from jax.experimental.pallas import tpu as pltpu

F32 = jnp.float32
BF16 = jnp.bfloat16

LANES = 128
NORM_EPS = 1e-6
NEG_INF = -1e30

D_MODEL = 2048
N_CHUNK = D_MODEL // LANES
N_PACK = N_CHUNK // 2
SGU_WIDTH = 512
SGU_GROUPS = 4
SGU_CHUNK = 128
ATT_HEADS = 4
ATT_HEAD_DIM = 128
ROPE_DIM = 32
ROPE_THETA = 500000.0
MOBA_BLOCK = 256
MOBA_TOPK = 3
SSM_WIDTH = 1024
SSM_HEAD_DIM = 64
SSM_HEADS = 16
SSM_GROUPS = 2
SSM_STATE = 128
SSM_CONV = 4
SSM_CHUNK = 128
SSM_CONV_DIM = 1536
MOE_GROUPS = 4
MOE_EPG = 8
MOE_EXPERTS = 32
MOE_FF = 512
MOE_BLOCK = 512

PROJ_MAIN = 5120
OFF_U, OFF_Z, OFF_V, OFF_Q, OFF_K, OFF_VA = 1536, 2048, 3072, 3584, 4096, 4608

VMEM_LIMIT = 56 * 1024 * 1024


def _params(sem, vmem=VMEM_LIMIT):
    return pltpu.CompilerParams(dimension_semantics=sem, vmem_limit_bytes=vmem)


def _silu(x):
    return x * jax.nn.sigmoid(x)


U32 = jnp.uint32
HI_MASK = 0xFFFF0000


def _pack_halves(lo, hi):
    lo_bits = lax.bitcast_convert_type(lo.astype(BF16).astype(F32), U32)
    hi_bits = lax.bitcast_convert_type(hi.astype(BF16).astype(F32), U32)
    return jnp.bitwise_or(jnp.bitwise_and(hi_bits, U32(HI_MASK)), jnp.right_shift(lo_bits, U32(16)))


def _unpack_halves(w):
    lo = lax.bitcast_convert_type(jnp.left_shift(w, U32(16)), F32)
    hi = lax.bitcast_convert_type(jnp.bitwise_and(w, U32(HI_MASK)), F32)
    return lo, hi


def _ada_kernel(c_ref, w_ref, b_ref, o_ref):
    ca = _silu(c_ref[...]).astype(BF16)
    o_ref[0] = jnp.dot(ca, w_ref[0].astype(BF16), preferred_element_type=F32) + b_ref[0]


def _ada_mod(c_pad, w_ada, b_ada):
    depth, d, n = w_ada.shape
    tn = 1024
    return pl.pallas_call(
        _ada_kernel,
        out_shape=jax.ShapeDtypeStruct((depth, c_pad.shape[0], n), F32),
        grid=(depth, n // tn),
        in_specs=[
            pl.BlockSpec(c_pad.shape, lambda l, j: (0, 0)),
            pl.BlockSpec((1, d, tn), lambda l, j: (l, 0, j)),
            pl.BlockSpec((1, 1, tn), lambda l, j: (l, 0, j)),
        ],
        out_specs=pl.BlockSpec((1, c_pad.shape[0], tn), lambda l, j: (l, 0, j)),
        compiler_params=_params(("parallel", "parallel")),
        name="ada_mod",
    )(c_pad, w_ada, b_ada.reshape(depth, 1, n))


def _modulated_norm(x, g, sc, sh):
    ms = jnp.mean(x * x, axis=-1, keepdims=True)
    return (x * lax.rsqrt(ms + NORM_EPS)) * g * (1.0 + sc) + sh


def _inproj_kernel(x_ref, g_ref, sc_ref, sh_ref, w_ref, wdt_ref, o_ref, dt_ref, h_ref):
    @pl.when(pl.program_id(2) == 0)
    def _():
        h = _modulated_norm(x_ref[0], g_ref[...], sc_ref[0], sh_ref[0]).astype(BF16)
        h_ref[...] = h
        dt_ref[0] = jnp.dot(h, wdt_ref[...], preferred_element_type=F32)

    o_ref[0] = jnp.dot(h_ref[...], w_ref[...], preferred_element_type=F32).astype(BF16)


def _in_proj(x, g, sc, sh, w, wdt):
    bsz, seq, d = x.shape
    n = w.shape[1]
    tm, tn = 512, 2560
    return pl.pallas_call(
        _inproj_kernel,
        out_shape=(jax.ShapeDtypeStruct((bsz, seq, n), BF16),
                   jax.ShapeDtypeStruct((bsz, seq, LANES), F32)),
        grid=(bsz, seq // tm, n // tn),
        in_specs=[
            pl.BlockSpec((1, tm, d), lambda b, i, j: (b, i, 0)),
            pl.BlockSpec((1, d), lambda b, i, j: (0, 0)),
            pl.BlockSpec((1, 1, d), lambda b, i, j: (b, 0, 0)),
            pl.BlockSpec((1, 1, d), lambda b, i, j: (b, 0, 0)),
            pl.BlockSpec((d, tn), lambda b, i, j: (0, j)),
            pl.BlockSpec((d, LANES), lambda b, i, j: (0, 0)),
        ],
        out_specs=(pl.BlockSpec((1, tm, tn), lambda b, i, j: (b, i, j)),
                   pl.BlockSpec((1, tm, LANES), lambda b, i, j: (b, i, 0))),
        scratch_shapes=[pltpu.VMEM((tm, d), BF16)],
        compiler_params=_params(("parallel", "parallel", "arbitrary")),
        name="in_proj",
    )(x, g, sc, sh, w, wdt)


def _sgu_kernel(u_ref, v_ref, g_ref, b_ref, w_ref, bs_ref, o_ref, *, n_chunks):
    row = lax.broadcasted_iota(jnp.int32, (SGU_CHUNK, SGU_CHUNK), 0)
    col = lax.broadcasted_iota(jnp.int32, (SGU_CHUNK, SGU_CHUNK), 1)
    w_causal = [jnp.where(row >= col, w_ref[g], 0.0).astype(BF16) for g in range(SGU_GROUPS)]
    for ci in range(n_chunks):
        rows = slice(ci * SGU_CHUNK, (ci + 1) * SGU_CHUNK)
        u = jax.nn.gelu(u_ref[0, rows, :].astype(F32))
        vf = jax.nn.gelu(v_ref[0, rows, :].astype(F32))
        mu = jnp.mean(vf, axis=-1, keepdims=True)
        var = jnp.mean(jnp.square(vf - mu), axis=-1, keepdims=True)
        vn = ((vf - mu) * lax.rsqrt(var + NORM_EPS) * g_ref[...] + b_ref[...]).astype(BF16)
        for g in range(SGU_GROUPS):
            cols = slice(g * LANES, (g + 1) * LANES)
            s = jnp.dot(w_causal[g], vn[:, cols], preferred_element_type=F32) + bs_ref[:, g:g + 1]
            o_ref[0, rows, cols] = (u[:, cols] * s).astype(BF16)


def _sgu(proj, ln_g, ln_b, w_s, b_s_t):
    bsz, seq, _ = proj.shape
    tc = 512
    blk = lambda off: pl.BlockSpec((1, tc, SGU_WIDTH), lambda b, i: (b, i, off // SGU_WIDTH))
    return pl.pallas_call(
        functools.partial(_sgu_kernel, n_chunks=tc // SGU_CHUNK),
        out_shape=jax.ShapeDtypeStruct((bsz, seq, SGU_WIDTH), BF16),
        grid=(bsz, seq // tc),
        in_specs=[
            blk(OFF_U), blk(OFF_V),
            pl.BlockSpec((1, SGU_WIDTH), lambda b, i: (0, 0)),
            pl.BlockSpec((1, SGU_WIDTH), lambda b, i: (0, 0)),
            pl.BlockSpec((SGU_GROUPS, SGU_CHUNK, SGU_CHUNK), lambda b, i: (0, 0, 0)),
            pl.BlockSpec((SGU_CHUNK, SGU_GROUPS), lambda b, i: (0, 0)),
        ],
        out_specs=pl.BlockSpec((1, tc, SGU_WIDTH), lambda b, i: (b, i, 0)),
        compiler_params=_params(("parallel", "parallel")),
        name="sgu",
    )(proj, proj, ln_g, ln_b, w_s, b_s_t)


def _rotary(t, c, s1, s2):
    half = ROPE_DIM // 2
    return t * c + pltpu.roll(t, LANES - half, 1) * s1 + pltpu.roll(t, half, 1) * s2


def _rope_kernel(q_ref, k_ref, v_ref, c_ref, s1_ref, s2_ref, qt_ref, kr_ref, vt_ref, km_ref):
    c, s1, s2 = c_ref[...], s1_ref[...], s2_ref[...]
    for h in range(ATT_HEADS):
        cols = slice(h * ATT_HEAD_DIM, (h + 1) * ATT_HEAD_DIM)
        q = _rotary(q_ref[0, :, cols].astype(F32), c, s1, s2)
        k = _rotary(k_ref[0, :, cols].astype(F32), c, s1, s2)
        qt_ref[0, h, 0] = q.T
        kr_ref[0, h, 0] = k.astype(BF16)
        vt_ref[0, h, 0] = v_ref[0, :, cols].astype(F32).T.astype(BF16)
        km_ref[0, h, 0] = jnp.mean(k, axis=0, keepdims=True)


def _rope_prep(proj, tab_c, tab_s1, tab_s2):
    bsz, seq, _ = proj.shape
    nblk = seq // MOBA_BLOCK
    hd, nh = ATT_HEAD_DIM, ATT_HEADS
    width = nh * hd
    sec = lambda off: pl.BlockSpec((1, MOBA_BLOCK, width), lambda i, b: (b, i, off // width))
    tab = pl.BlockSpec((MOBA_BLOCK, hd), lambda i, b: (i, 0))
    out = lambda rows, cols: pl.BlockSpec((1, nh, 1, rows, cols), lambda i, b: (b, 0, i, 0, 0))
    return pl.pallas_call(
        _rope_kernel,
        out_shape=(
            jax.ShapeDtypeStruct((bsz, nh, nblk, hd, MOBA_BLOCK), F32),
            jax.ShapeDtypeStruct((bsz, nh, nblk, MOBA_BLOCK, hd), BF16),
            jax.ShapeDtypeStruct((bsz, nh, nblk, hd, MOBA_BLOCK), BF16),
            jax.ShapeDtypeStruct((bsz, nh, nblk, 1, hd), F32),
        ),
        grid=(nblk, bsz),
        in_specs=[sec(OFF_Q), sec(OFF_K), sec(OFF_VA), tab, tab, tab],
        out_specs=(out(hd, MOBA_BLOCK), out(MOBA_BLOCK, hd), out(hd, MOBA_BLOCK), out(1, hd)),
        compiler_params=_params(("parallel", "parallel")),
        name="rope_prep",
    )(proj, proj, proj, tab_c, tab_s1, tab_s2)


LOG2E = 1.4426950408889634
SOFTMAX_ROWS = 64


def _moba_kernel(qt_ref, kr_ref, vt_ref, km_ref, o_ref, sel_ref, qs_ref, acc_ref, p_ref, s_ref, *, nblk):
    i = pl.program_id(1)
    tq, nh, hd = MOBA_BLOCK, ATT_HEADS, ATT_HEAD_DIM
    blk = lax.broadcasted_iota(jnp.int32, (nblk, tq), 0)
    valid = blk < i
    blk_f = blk.astype(F32)
    krow = lax.broadcasted_iota(jnp.int32, (tq, tq), 0)
    qcol = lax.broadcasted_iota(jnp.int32, (tq, tq), 1)

    stats = []
    for h in range(nh):
        qt = qt_ref[0, h, 0]
        qs_ref[h] = (qt * (hd ** -0.5 * LOG2E)).astype(BF16)

        gate = jnp.dot(km_ref[0, h], qt, preferred_element_type=F32,
                       precision=lax.Precision.HIGHEST)
        gate = jnp.where(valid, gate, NEG_INF)
        picked = jnp.zeros((nblk, tq), jnp.bool_)
        for _ in range(MOBA_TOPK):
            top = jnp.max(gate, axis=0, keepdims=True)
            first = jnp.min(jnp.where(gate == top, blk_f, float(nblk)), axis=0, keepdims=True)
            pick = blk_f == first
            picked = picked | pick
            gate = jnp.where(pick, -jnp.inf, gate)
        sel_ref[h] = jnp.where(valid & picked, 1.0, 0.0)

        s = jnp.dot(kr_ref[0, h, i], qs_ref[h], preferred_element_type=F32)
        s = jnp.where(krow <= qcol, s, NEG_INF)
        m0 = jnp.max(s, axis=0, keepdims=True)
        p = jnp.exp2(s - m0)
        acc_ref[h] = jnp.dot(vt_ref[0, h, i], p.astype(BF16), preferred_element_type=F32)
        stats += [m0, jnp.sum(p, axis=0, keepdims=True)]

    def put_scores(slot, j):
        for h in range(nh):
            s = jnp.dot(kr_ref[0, h, j], qs_ref[h], preferred_element_type=F32)
            s_ref[slot, h] = jnp.where(sel_ref[h, pl.ds(j, 1), :] > 0.5, s, NEG_INF)

    def consume(slot, j, stats):
        out = []
        for h in range(nh):
            m, l = stats[2 * h], stats[2 * h + 1]
            m_new = jnp.maximum(m, jnp.max(s_ref[slot, h], axis=0, keepdims=True))
            alpha = jnp.exp2(m - m_new)
            l = alpha * l
            for r in range(0, tq, SOFTMAX_ROWS):
                p = jnp.exp2(s_ref[slot, h, r:r + SOFTMAX_ROWS, :] - m_new)
                l = l + jnp.sum(p, axis=0, keepdims=True)
                p_ref[h, r:r + SOFTMAX_ROWS, :] = p.astype(BF16)
            acc_ref[h] = alpha * acc_ref[h] + jnp.dot(vt_ref[0, h, j], p_ref[h],
                                                      preferred_element_type=F32)
            out += [m_new, l]
        return out

    put_scores(0, 0)

    def body(t, stats):
        first = 2 * t
        second = jnp.minimum(first + 1, nblk - 1)
        put_scores(1, second)
        stats = consume(0, first, stats)
        put_scores(0, jnp.minimum(first + 2, nblk - 1))
        return tuple(consume(1, second, stats))

    stats = lax.fori_loop(0, (i + 1) // 2, body, tuple(stats))
    for h in range(nh):
        o_ref[0, :, h * hd:(h + 1) * hd] = (acc_ref[h] / stats[2 * h + 1]).T.astype(BF16)


def _moba(qt, kr, vt, km):
    bsz, nh, nblk, hd, tq = qt.shape
    seq = nblk * tq
    full = lambda shape: pl.BlockSpec((1, nh) + shape, lambda b, i: (b, 0) + (0,) * len(shape))
    return pl.pallas_call(
        functools.partial(_moba_kernel, nblk=nblk),
        out_shape=jax.ShapeDtypeStruct((bsz, seq, nh * hd), BF16),
        grid=(bsz, nblk),
        in_specs=[
            pl.BlockSpec((1, nh, 1, hd, tq), lambda b, i: (b, 0, i, 0, 0)),
            full((nblk, tq, hd)), full((nblk, hd, tq)), full((nblk, hd)),
        ],
        out_specs=pl.BlockSpec((1, tq, nh * hd), lambda b, i: (b, i, 0)),
        scratch_shapes=[pltpu.VMEM((nh, nblk, tq), F32), pltpu.VMEM((nh, hd, tq), BF16),
                        pltpu.VMEM((nh, hd, tq), F32), pltpu.VMEM((nh, tq, tq), BF16),
                        pltpu.VMEM((2, nh, tq, tq), F32)],
        compiler_params=_params(("parallel", "arbitrary")),
        name="moba",
    )(qt, kr, vt, km)


def _softplus(x):
    return jnp.maximum(x, 0.0) + jnp.log1p(jnp.exp(-jnp.abs(x)))


def _ssd_kernel(xbc_ref, z_ref, dt_ref, cw_ref, cb_ref, dtb_ref, alog_ref, dsk_ref, nw_ref,
                o_ref, prev_ref, state_ref, y_ref):
    L, P, N = SSM_CHUNK, SSM_HEAD_DIM, SSM_STATE
    hi = lax.Precision.HIGHEST

    @pl.when(pl.program_id(1) == 0)
    def _():
        prev_ref[...] = jnp.zeros_like(prev_ref)
        state_ref[...] = jnp.zeros_like(state_ref)

    cur = xbc_ref[0].astype(F32)
    prev = prev_ref[...]
    crow = lax.broadcasted_iota(jnp.int32, cur.shape, 0)
    acc = cur * cw_ref[SSM_CONV - 1:SSM_CONV, :] + cb_ref[...]
    for j in range(1, SSM_CONV):
        shifted = pltpu.roll(jnp.where(crow >= L - j, prev, cur), j, 0)
        acc = acc + shifted * cw_ref[SSM_CONV - 1 - j:SSM_CONV - j, :]
    prev_ref[...] = cur
    xc = _silu(acc)
    xs = xc[:, :SSM_WIDTH]
    bm = xc[:, SSM_WIDTH:SSM_WIDTH + SSM_GROUPS * N].astype(BF16)
    cm = xc[:, SSM_WIDTH + SSM_GROUPS * N:].astype(BF16)
    xs_b = xs.astype(BF16)
    xs_t = xs.T

    dt = _softplus(dt_ref[0] + dtb_ref[...])
    da = dt * (-jnp.exp(alog_ref[...]))
    row = lax.broadcasted_iota(jnp.int32, (L, L), 0)
    col = lax.broadcasted_iota(jnp.int32, (L, L), 1)
    causal = row >= col
    acum = jnp.dot(jnp.where(causal, 1.0, 0.0), da, preferred_element_type=F32, precision=hi)
    acum_t = jnp.dot(da.T, jnp.where(row <= col, 1.0, 0.0), preferred_element_type=F32, precision=hi)
    dt_t = dt.T
    last_t = acum_t[:, L - 1:L]
    w_end_t = jnp.exp(last_t - acum_t) * dt_t
    e_last_t = jnp.exp(last_t)
    e_acum = jnp.exp(acum)

    for g in range(SSM_GROUPS):
        b_g = bm[:, g * N:(g + 1) * N]
        c_g = cm[:, g * N:(g + 1) * N]
        cb = lax.dot_general(c_g, b_g, (((1,), (1,)), ((), ())), preferred_element_type=F32)
        for hh in range(SSM_HEADS // SSM_GROUPS):
            h = g * (SSM_HEADS // SSM_GROUPS) + hh
            hs = slice(h * P, (h + 1) * P)
            decay = jnp.exp(jnp.where(causal, acum[:, h:h + 1] - acum_t[h:h + 1, :], -jnp.inf))
            mat = (cb * decay * dt_t[h:h + 1, :]).astype(BF16)
            y = jnp.dot(mat, xs_b[:, hs], preferred_element_type=F32)
            st = state_ref[hs, :]
            y = y + lax.dot_general(c_g, st.astype(BF16), (((1,), (1,)), ((), ())),
                                    preferred_element_type=F32) * e_acum[:, h:h + 1]
            y_ref[:, hs] = y
            xw = (xs_t[hs, :] * w_end_t[h:h + 1, :]).astype(BF16)
            state_ref[hs, :] = st * e_last_t[h:h + 1, :] + jnp.dot(xw, b_g, preferred_element_type=F32)

    y = y_ref[...] + dsk_ref[...] * xs
    y = y * _silu(z_ref[0].astype(F32))
    gw = SSM_WIDTH // SSM_GROUPS
    for g in range(SSM_GROUPS):
        cols = slice(g * gw, (g + 1) * gw)
        yg = y[:, cols]
        yg = yg * lax.rsqrt(jnp.mean(yg * yg, axis=-1, keepdims=True) + NORM_EPS)
        o_ref[0, :, cols] = (yg * nw_ref[:, cols]).astype(BF16)


def _ssd(proj, dt_raw, conv_w, conv_b, dt_bias, a_log, d_skip, ssm_norm):
    bsz, seq, _ = proj.shape
    L = SSM_CHUNK
    vec = lambda n: pl.BlockSpec((1, n), lambda b, c: (0, 0))
    return pl.pallas_call(
        _ssd_kernel,
        out_shape=jax.ShapeDtypeStruct((bsz, seq, SSM_WIDTH), BF16),
        grid=(bsz, seq // L),
        in_specs=[
            pl.BlockSpec((1, L, SSM_CONV_DIM), lambda b, c: (b, c, 0)),
            pl.BlockSpec((1, L, SSM_WIDTH), lambda b, c: (b, c, OFF_Z // SSM_WIDTH)),
            pl.BlockSpec((1, L, LANES), lambda b, c: (b, c, 0)),
            pl.BlockSpec((SSM_CONV, SSM_CONV_DIM), lambda b, c: (0, 0)),
            vec(SSM_CONV_DIM), vec(LANES), vec(LANES), vec(SSM_WIDTH), vec(SSM_WIDTH),
        ],
        out_specs=pl.BlockSpec((1, L, SSM_WIDTH), lambda b, c: (b, c, 0)),
        scratch_shapes=[
            pltpu.VMEM((L, SSM_CONV_DIM), F32),
            pltpu.VMEM((SSM_WIDTH, SSM_STATE), F32),
            pltpu.VMEM((L, SSM_WIDTH), F32),
        ],
        compiler_params=_params(("parallel", "arbitrary")),
        name="ssd",
    )(proj, proj, dt_raw, conv_w, conv_b, dt_bias, a_log, d_skip, ssm_norm)


def _route(logits):
    lane = lax.broadcasted_iota(jnp.int32, logits.shape, 1).astype(F32)
    big = float(LANES)

    def first_argmax(vals, mask):
        top = jnp.max(vals, axis=-1, keepdims=True)
        idx = jnp.min(jnp.where(mask & (vals >= top), lane, big), axis=-1, keepdims=True)
        return top, idx

    def masked_softmax(mask):
        z = jnp.where(mask, logits, -jnp.inf)
        e = jnp.exp(z - jnp.max(z, axis=-1, keepdims=True))
        return e / jnp.sum(e, axis=-1, keepdims=True)

    gmask = lane < MOE_GROUPS
    g_prob, g_idx = first_argmax(jnp.where(gmask, masked_softmax(gmask), -1.0), gmask)
    lo = MOE_GROUPS + g_idx * MOE_EPG
    fmask = (lane >= lo) & (lane < lo + MOE_EPG)
    p_exp = jnp.where(fmask, masked_softmax(fmask), -1.0)
    p1, i1 = first_argmax(p_exp, fmask)
    rest = fmask & (lane != i1)
    p2, i2 = first_argmax(jnp.where(rest, p_exp, -1.0), rest)
    den = p1 + p2
    out = jnp.where(lane == 0, i1 - MOE_GROUPS, 0.0)
    out = jnp.where(lane == 1, i2 - MOE_GROUPS, out)
    out = jnp.where(lane == 2, g_prob * p1 / den, out)
    out = jnp.where(lane == 3, g_prob * p2 / den, out)
    return out


def _outproj_kernel(ya_ref, yb_ref, yc_ref, x_ref, g1_ref, wa_ref, wb_ref, wc_ref,
                    n2_ref, sc_ref, sh_ref, wr_ref, br_ref, xo_ref, hc_ref, rt_ref):
    mix = jnp.dot(ya_ref[0], wa_ref[...], preferred_element_type=F32)
    mix = mix + jnp.dot(yb_ref[0], wb_ref[...], preferred_element_type=F32)
    mix = mix + jnp.dot(yc_ref[0], wc_ref[...], preferred_element_type=F32)
    xn = x_ref[0] + g1_ref[0] * mix
    xo_ref[0] = xn
    h = _modulated_norm(xn, n2_ref[...], sc_ref[0], sh_ref[0])
    half = N_PACK * LANES
    packed = _pack_halves(h[:, :half], h[:, half:])
    for c in range(N_PACK):
        hc_ref[0, c] = packed[:, c * LANES:(c + 1) * LANES]
    h_hi = h.astype(BF16)
    h_lo = (h - h_hi.astype(F32)).astype(BF16)
    part = jnp.dot(h_hi, wr_ref[...], preferred_element_type=F32)
    logits = (part[:, :LANES] + part[:, LANES:]
              + jnp.dot(h_lo, wr_ref[:, :LANES], preferred_element_type=F32) + br_ref[...])
    rt_ref[0] = _route(logits)


TM_TOK = 512


def _out_proj_route(ya, yb, yc, x, g1, wa, wb, wc, n2, sc2, sh2, w_r, b_r):
    bsz, seq, d = x.shape
    tm = TM_TOK
    nt = seq // tm
    row = lambda w: pl.BlockSpec((1, tm, w), lambda b, i: (b, i, 0))
    per_b = pl.BlockSpec((1, 1, d), lambda b, i: (b, 0, 0))
    const = lambda a: pl.BlockSpec(a.shape, lambda b, i: (0,) * a.ndim,
                                   pipeline_mode=pl.Buffered(1))
    return pl.pallas_call(
        _outproj_kernel,
        out_shape=(jax.ShapeDtypeStruct((bsz, seq, d), F32),
                   jax.ShapeDtypeStruct((bsz * nt, N_PACK, tm, LANES), U32),
                   jax.ShapeDtypeStruct((bsz, seq, LANES), F32)),
        grid=(bsz, nt),
        in_specs=[row(ya.shape[-1]), row(yb.shape[-1]), row(yc.shape[-1]), row(d), per_b,
                  const(wa), const(wb), const(wc), const(n2), per_b, per_b, const(w_r), const(b_r)],
        out_specs=(row(d),
                   pl.BlockSpec((1, N_PACK, tm, LANES), lambda b, i: (b * nt + i, 0, 0, 0)),
                   row(LANES)),
        compiler_params=_params(("parallel", "parallel")),
        name="out_proj_route",
    )(ya, yb, yc, x, g1, wa, wb, wc, n2, sc2, sh2, w_r, b_r)


BLOCK_SHIFT = MOE_BLOCK.bit_length() - 1


TOK_SHIFT = TM_TOK.bit_length() - 1
MXU_WIDTH = 256
GATHER_UNROLL = 8


def _expert_kernel(be_ref, nv_ref, tok_ref, h_hbm, wg_ref, wu_ref, wd_ref, o_ref,
                   buf_ref, x_ref, wgb_ref, wub_ref, wdb_ref, sem):
    i = pl.program_id(0)
    slot = i % 2
    half = N_PACK * LANES

    def gather(blk, slot_, mi):
        tok = tok_ref[blk * MOE_BLOCK + mi]
        return pltpu.make_async_copy(
            h_hbm.at[jnp.right_shift(tok, TOK_SHIFT), :, jnp.bitwise_and(tok, TM_TOK - 1), :],
            buf_ref.at[slot_, :, mi, :], sem.at[slot_])

    def for_valid_rows(blk, fn):
        def body(r, carry):
            for u in range(GATHER_UNROLL):
                fn(r * GATHER_UNROLL + u)
            return carry
        lax.fori_loop(0, (nv_ref[blk] + GATHER_UNROLL - 1) // GATHER_UNROLL, body, 0)

    @pl.when(i == 0)
    def _():
        buf_ref[...] = jnp.zeros_like(buf_ref)
        for_valid_rows(0, lambda mi: gather(0, 0, mi).start())

    @pl.when(i + 1 < pl.num_programs(0))
    def _():
        for_valid_rows(i + 1, lambda mi: gather(i + 1, 1 - slot, mi).start())

    @pl.when((i == 0) | (be_ref[i] != be_ref[jnp.maximum(i - 1, 0)]))
    def _():
        wgb_ref[...] = wg_ref[0, 0].astype(BF16)
        wub_ref[...] = wu_ref[0, 0].astype(BF16)
        wdb_ref[...] = wd_ref[0, 0].astype(BF16)

    for_valid_rows(i, lambda mi: gather(i, slot, mi).wait())

    @pl.when(nv_ref[i] == 0)
    def _():
        o_ref[...] = jnp.zeros_like(o_ref)

    @pl.when(nv_ref[i] > 0)
    def _():
        for c in range(N_PACK):
            lo, hi = _unpack_halves(buf_ref[slot, c])
            x_ref[:, c * LANES:(c + 1) * LANES] = lo.astype(BF16)
            x_ref[:, half + c * LANES:half + (c + 1) * LANES] = hi.astype(BF16)
        ff = wgb_ref.shape[1]
        hid = []
        for p in range(ff // MXU_WIDTH):
            cols = slice(p * MXU_WIDTH, (p + 1) * MXU_WIDTH)
            gate = jnp.dot(x_ref[...], wgb_ref[:, cols], preferred_element_type=F32)
            up = jnp.dot(x_ref[...], wub_ref[:, cols], preferred_element_type=F32)
            hid.append((_silu(gate) * up).astype(BF16))
        hid = jnp.concatenate(hid, axis=-1)
        for p in range(half // MXU_WIDTH):
            cols = slice(p * MXU_WIDTH, (p + 1) * MXU_WIDTH)
            y_lo = jnp.dot(hid, wdb_ref[:, cols], preferred_element_type=F32)
            y_hi = jnp.dot(hid, wdb_ref[:, half + p * MXU_WIDTH:half + (p + 1) * MXU_WIDTH],
                           preferred_element_type=F32)
            packed = _pack_halves(y_lo, y_hi)
            for c in range(MXU_WIDTH // LANES):
                o_ref[0, p * (MXU_WIDTH // LANES) + c] = packed[:, c * LANES:(c + 1) * LANES]


def _experts(layer, blk_exp, blk_rows, row_tok, h_chunks, w_gate, w_up, w_down):
    n_blocks = blk_exp.shape[0]
    d, ff = w_gate.shape[2], w_gate.shape[3]
    return pl.pallas_call(
        _expert_kernel,
        out_shape=jax.ShapeDtypeStruct((n_blocks, N_PACK, MOE_BLOCK, LANES), U32),
        grid_spec=pltpu.PrefetchScalarGridSpec(
            num_scalar_prefetch=3,
            grid=(n_blocks,),
            in_specs=[pl.BlockSpec(memory_space=pl.ANY)] + [
                pl.BlockSpec(shape, lambda i, be, nv, rt: (layer, be[i], 0, 0))
                for shape in ((1, 1, d, ff), (1, 1, d, ff), (1, 1, ff, d))],
            out_specs=pl.BlockSpec((1, N_PACK, MOE_BLOCK, LANES), lambda i, be, nv, rt: (i, 0, 0, 0)),
            scratch_shapes=[pltpu.VMEM((2, N_PACK, MOE_BLOCK, LANES), U32),
                            pltpu.VMEM((MOE_BLOCK, d), BF16),
                            pltpu.VMEM((d, ff), BF16), pltpu.VMEM((d, ff), BF16),
                            pltpu.VMEM((ff, d), BF16), pltpu.SemaphoreType.DMA((2,))],
        ),
        compiler_params=_params(("arbitrary",)),
        name="moe_experts",
    )(blk_exp, blk_rows, row_tok, h_chunks, w_gate, w_up, w_down)


def _combine_kernel(dest_ref, y_hbm, x_ref, g2_ref, rt_ref, fn_ref, o_ref, buf_ref, sem, *, final):
    i = pl.program_id(0)
    slot = i % 2
    tm = TM_TOK

    def gather(tile, slot_, k, mi):
        d = dest_ref[(tile * tm + mi) * 2 + k]
        return pltpu.make_async_copy(
            y_hbm.at[jnp.right_shift(d, BLOCK_SHIFT), :, jnp.bitwise_and(d, MOE_BLOCK - 1), :],
            buf_ref.at[slot_, k, :, mi, :], sem.at[slot_])

    def for_rows(fn):
        def body(r, carry):
            for u in range(GATHER_UNROLL // 2):
                for k in range(2):
                    fn(k, r * (GATHER_UNROLL // 2) + u)
            return carry
        lax.fori_loop(0, tm // (GATHER_UNROLL // 2), body, 0)

    @pl.when(i == 0)
    def _():
        for_rows(lambda k, mi: gather(0, 0, k, mi).start())

    @pl.when(i + 1 < pl.num_programs(0))
    def _():
        for_rows(lambda k, mi: gather(i + 1, 1 - slot, k, mi).start())

    for_rows(lambda k, mi: gather(i, slot, k, mi).wait())
    rt = rt_ref[...]
    gate0, gate1 = rt[:, 2:3], rt[:, 3:4]
    for c in range(N_PACK):
        halves = zip(_unpack_halves(buf_ref[slot, 0, c]), _unpack_halves(buf_ref[slot, 1, c]))
        for part, (y0, y1) in enumerate(halves):
            cols = slice((part * N_PACK + c) * LANES, (part * N_PACK + c + 1) * LANES)
            o_ref[:, cols] = x_ref[:, cols] + g2_ref[0, :, cols] * (y0 * gate0 + y1 * gate1)
    if final:
        x = o_ref[...]
        ms = jnp.mean(x * x, axis=-1, keepdims=True)
        o_ref[...] = (x * lax.rsqrt(ms + NORM_EPS)) * fn_ref[...]


def _combine(dest_flat, y_rows, x2d, g2, route2d, seq, final_g, final):
    n_tok, d = x2d.shape
    tm = TM_TOK
    per_seq = seq // tm
    return pl.pallas_call(
        functools.partial(_combine_kernel, final=final),
        out_shape=jax.ShapeDtypeStruct((n_tok, d), F32),
        grid_spec=pltpu.PrefetchScalarGridSpec(
            num_scalar_prefetch=1,
            grid=(n_tok // tm,),
            in_specs=[pl.BlockSpec(memory_space=pl.ANY),
                      pl.BlockSpec((tm, d), lambda i, dr: (i, 0)),
                      pl.BlockSpec((1, 1, d), lambda i, dr: (i // per_seq, 0, 0)),
                      pl.BlockSpec((tm, LANES), lambda i, dr: (i, 0)),
                      pl.BlockSpec((1, d), lambda i, dr: (0, 0))],
            out_specs=pl.BlockSpec((tm, d), lambda i, dr: (i, 0)),
            scratch_shapes=[pltpu.VMEM((2, 2, N_PACK, tm, LANES), U32),
                            pltpu.SemaphoreType.DMA((2,))],
        ),
        compiler_params=_params(("arbitrary",)),
        name="moe_combine",
    )(dest_flat, y_rows, x2d, g2, route2d, final_g)


def _dispatch_plan(route2d):
    n_tok = route2d.shape[0]
    n_asg = n_tok * 2
    flat_e = route2d[:, 0:2].astype(jnp.int32).reshape(n_asg)
    asg = jnp.arange(n_asg, dtype=jnp.int32)
    s_e, order = lax.sort((flat_e, asg), num_keys=1)
    ids = jnp.arange(MOE_EXPERTS, dtype=jnp.int32)
    start = jnp.sum(s_e[:, None] < ids[None, :], axis=0, dtype=jnp.int32)
    counts = jnp.sum(s_e[:, None] == ids[None, :], axis=0, dtype=jnp.int32)
    pcounts = (counts + MOE_BLOCK - 1) // MOE_BLOCK * MOE_BLOCK
    pend = jnp.cumsum(pcounts)
    pstart = pend - pcounts
    dest_sorted = pstart[s_e] + asg - start[s_e]
    _, dest = lax.sort((order, dest_sorted), num_keys=1)
    n_blocks = n_asg // MOE_BLOCK + MOE_EXPERTS
    blk_start = jnp.arange(n_blocks, dtype=jnp.int32) * MOE_BLOCK
    blk_exp = jnp.minimum(jnp.sum(pend[None, :] <= blk_start[:, None], axis=1, dtype=jnp.int32),
                          MOE_EXPERTS - 1)
    row = jnp.arange(n_blocks * MOE_BLOCK, dtype=jnp.int32)
    row_e = jnp.repeat(blk_exp, MOE_BLOCK)
    off = row - pstart[row_e]
    src = jnp.minimum(start[row_e] + off, n_asg - 1)
    row_tok = jnp.where(off < counts[row_e], order[src] // 2, 0).astype(jnp.int32)
    blk_rows = jnp.clip(counts[blk_exp] - (blk_start - pstart[blk_exp]), 0, MOE_BLOCK).astype(jnp.int32)
    return dest, row_tok, blk_exp, blk_rows


def _rotary_tables(seq):
    pos = jnp.arange(seq, dtype=F32)
    inv_freq = ROPE_THETA ** (-jnp.arange(0, ROPE_DIM, 2, dtype=F32) / ROPE_DIM)
    ang = pos[:, None] * inv_freq[None, :]
    cos, sin = jnp.cos(ang), jnp.sin(ang)
    half = ROPE_DIM // 2
    ones = jnp.ones((seq, LANES - ROPE_DIM), F32)
    tab_c = jnp.concatenate([cos, cos, ones], axis=-1)
    tab_s1 = jnp.concatenate([-sin, jnp.zeros((seq, LANES - half), F32)], axis=-1)
    tab_s2 = jnp.concatenate([jnp.zeros((seq, half), F32), sin,
                              jnp.zeros((seq, LANES - ROPE_DIM), F32)], axis=-1)
    return tab_c, tab_s1, tab_s2


def _pad_lanes(v):
    return jnp.pad(v, (0, LANES - v.shape[0])).reshape(1, LANES)


def kernel(x, c, norm1, norm2, w_ada, b_ada, w_in, w_out, sgu_ln_g, sgu_ln_b, sgu_w, sgu_b, conv_w, conv_b, dt_bias, a_log, d_skip, ssm_norm, w_coarse, b_coarse, w_fine, b_fine, w_gate, w_up, w_down, final_norm):
    bsz, seq, d = x.shape
    depth = w_in.shape[0]
    n_tok = bsz * seq

    c_pad = jnp.pad(c, ((0, 8 - bsz % 8 if bsz % 8 else 0), (0, 0)))
    mod = _ada_mod(c_pad, w_ada, b_ada)[:, :bsz].reshape(depth, bsz, 6, 1, d)
    tab_c, tab_s1, tab_s2 = _rotary_tables(seq)

    for l in range(depth):
        sh1, sc1, g1, sh2, sc2, g2 = [mod[l, :, i] for i in range(6)]
        wl = w_in[l]
        w_main = jnp.concatenate(
            [wl[:, 3584:5120], wl[:, 0:512], wl[:, 2560:3584], wl[:, 512:1024], wl[:, 1024:2560]],
            axis=1).astype(BF16)
        w_dt = jnp.pad(wl[:, 5120:], ((0, 0), (0, LANES - SSM_HEADS))).astype(BF16)
        proj, dt_raw = _in_proj(x, norm1[l].reshape(1, d), sc1, sh1, w_main, w_dt)

        ya = _sgu(proj, sgu_ln_g[l].reshape(1, -1), sgu_ln_b[l].reshape(1, -1), sgu_w[l], sgu_b[l].T)
        qt, kr, vt, km = _rope_prep(proj, tab_c, tab_s1, tab_s2)
        yb = _moba(qt, kr, vt, km.reshape(km.shape[:3] + (ATT_HEAD_DIM,)))
        yc = _ssd(proj, dt_raw, conv_w[l], conv_b[l].reshape(1, -1), _pad_lanes(dt_bias[l]),
                  _pad_lanes(a_log[l]), jnp.repeat(d_skip[l], SSM_HEAD_DIM).reshape(1, -1),
                  ssm_norm[l].reshape(1, -1))

        wo = w_out[l].astype(BF16)
        w_r = jnp.concatenate([w_coarse[l], w_fine[l].transpose(1, 0, 2).reshape(d, MOE_EXPERTS)], axis=1)
        w_r = jnp.pad(w_r, ((0, 0), (0, LANES - w_r.shape[1])))
        w_r_hi = w_r.astype(BF16)
        w_r = jnp.concatenate([w_r_hi, (w_r - w_r_hi.astype(F32)).astype(BF16)], axis=1)
        b_r = _pad_lanes(jnp.concatenate([b_coarse[l], b_fine[l].reshape(-1)]))
        x, h_chunks, route = _out_proj_route(
            ya, yb, yc, x, g1, wo[:SGU_WIDTH], wo[SGU_WIDTH:2 * SGU_WIDTH], wo[2 * SGU_WIDTH:],
            norm2[l].reshape(1, d), sc2, sh2, w_r, b_r)

        route2d = route.reshape(n_tok, LANES)
        dest, row_tok, blk_exp, blk_rows = _dispatch_plan(route2d)
        y_rows = _experts(l, blk_exp, blk_rows, row_tok, h_chunks, w_gate, w_up, w_down)
        x = _combine(dest, y_rows, x.reshape(n_tok, d), g2, route2d, seq,
                     final_norm.reshape(1, d), final=(l == depth - 1)).reshape(bsz, seq, d)

    return x
```
